```python
import jax, jax.numpy as jnp
from jax import lax
import numpy as np

D_MODEL = 1024
BATCH = 8
SEQ = 4096
DEPTH = 1

N_MEM = 256
EPS = 1e-6
CONV_WIDTH = D_MODEL
CONV_K = 31
ATT_HEADS = 8
ATT_HEAD_DIM = D_MODEL // ATT_HEADS
ATT_WIDTH = ATT_HEADS * ATT_HEAD_DIM
IDX_HEADS = 8
IDX_HEAD_DIM = 64
TOPK_MAX = 256
Q_BLOCK = 32
MEM_HEADS = 4
MEM_HEAD_DIM = D_MODEL // MEM_HEADS
MEM_WIDTH = MEM_HEADS * MEM_HEAD_DIM
N_BRANCH = 3
FFN_DIM = 2816
FFN_CONV_K = 3
IN_SPLITS = (2 * CONV_WIDTH, ATT_WIDTH, ATT_WIDTH, ATT_WIDTH,
             IDX_HEADS * IDX_HEAD_DIM, IDX_HEADS, IDX_HEAD_DIM, MEM_WIDTH)
IN_WIDTH = sum(IN_SPLITS)

kernel_name = 'hybrid_conv_dsa_memory_block'


def rms_norm(x, g):
    xf = x.astype(jnp.float32)
    y = xf * lax.rsqrt(jnp.mean(xf * xf, axis=-1, keepdims=True) + EPS)
    return (y * g.astype(jnp.float32)).astype(x.dtype)


def layer_norm(x, g, b):
    xf = x.astype(jnp.float32)
    mu = jnp.mean(xf, axis=-1, keepdims=True)
    xc = xf - mu
    var = jnp.mean(xc * xc, axis=-1, keepdims=True)
    y = xc * lax.rsqrt(var + EPS) * g.astype(jnp.float32) + b.astype(jnp.float32)
    return y.astype(x.dtype)


def causal_depthwise_conv(x, w, b):
    width, c = w.shape
    y = lax.conv_general_dilated(
        x, w[:, None, :].astype(x.dtype), window_strides=(1,),
        padding=[(width - 1, 0)], dimension_numbers=('NWC', 'WIO', 'NWC'),
        feature_group_count=c)
    return y + b.astype(x.dtype)


def conformer_conv(u, dw_w, dw_b, ln_g, ln_b, pw2):
    a, gate = jnp.split(u, 2, axis=-1)
    y = a * jax.nn.sigmoid(gate)
    y = causal_depthwise_conv(y, dw_w, dw_b)
    y = jax.nn.silu(layer_norm(y, ln_g, ln_b))
    return y @ pw2


def dsa_sparse_attention(q, k, v, q_idx, w_idx, k_idx):
    b, s, h, dh = q.shape
    topk = min(TOPK_MAX, s // 4)
    nb = s // Q_BLOCK
    kv = jnp.concatenate([k, v], axis=-1)
    k_idx_f = k_idx.astype(jnp.float32)
    key_pos = jnp.arange(s, dtype=jnp.int32)
    q_pos = key_pos.reshape(nb, Q_BLOCK)
    idx_scale = (IDX_HEADS ** -0.5) * (IDX_HEAD_DIM ** -0.5)
    att_scale = dh ** -0.5

    def to_blocks(a):
        return a.reshape((b, nb, Q_BLOCK) + a.shape[2:]).swapaxes(0, 1)

    def block(args):
        qb, qib, wb, tb = args
        dots = jnp.einsum('bthd,bsd->bths', qib.astype(jnp.float32), k_idx_f)
        score = jnp.einsum('bths,bth->bts', jax.nn.relu(dots),
                           wb.astype(jnp.float32)) * idx_scale
        causal = key_pos[None, :] <= tb[:, None]
        score = jnp.where(causal[None], score, -jnp.inf)
        _, sel = lax.top_k(score, topk)
        valid = sel <= tb[None, :, None]
        kv_sel = jax.vmap(lambda kvb, ib: kvb[ib])(kv, sel)
        k_sel, v_sel = jnp.split(kv_sel, 2, axis=-1)
        logits = jnp.einsum('bthd,btkhd->bthk', qb, k_sel).astype(jnp.float32) * att_scale
        logits = jnp.where(valid[:, :, None, :], logits, -jnp.inf)
        p = jax.nn.softmax(logits, axis=-1).astype(v_sel.dtype)
        return jnp.einsum('bthk,btkhd->bthd', p, v_sel)

    out = lax.map(block, (to_blocks(q), to_blocks(q_idx), to_blocks(w_idx), q_pos))
    return out.swapaxes(0, 1).reshape(b, s, h * dh)


def memory_attention(q, mk, mv):
    b, s, hm, dm = q.shape
    logits = jnp.einsum('bshd,bmhd->bhsm', q, mk).astype(jnp.float32) * (dm ** -0.5)
    p = jax.nn.softmax(logits, axis=-1).astype(mv.dtype)
    return jnp.einsum('bhsm,bmhd->bshd', p, mv).reshape(b, s, hm * dm)


def setup_inputs(seed: int = 0) -> dict:
    key = jax.random.key(seed)
    ks = iter(jax.random.split(key, 32))

    def nrm(shape, scale):
        return jax.random.normal(next(ks), shape, jnp.float32) * scale

    def gain(shape):
        return 1.0 + nrm(shape, 0.05)

    L, D = DEPTH, D_MODEL
    return {
        'x': nrm((BATCH, SEQ, D), 1.0),
        'mem': nrm((BATCH, N_MEM, D), 1.0),
        'norm1_pre_g': gain((L, D)),
        'w_in': nrm((L, D, IN_WIDTH), D ** -0.5),
        'conv_dw_w': nrm((L, CONV_K, CONV_WIDTH), CONV_K ** -0.5),
        'conv_dw_b': nrm((L, CONV_WIDTH), 0.02),
        'conv_ln_g': gain((L, CONV_WIDTH)),
        'conv_ln_b': nrm((L, CONV_WIDTH), 0.02),
        'conv_pw2': nrm((L, CONV_WIDTH, D), CONV_WIDTH ** -0.5),
        'mem_norm_g': gain((L, D)),
        'w_mem_kv': nrm((L, D, 2 * MEM_WIDTH), D ** -0.5),
        'w_gate': nrm((L, D, N_BRANCH * D), D ** -0.5),
        'b_gate': nrm((L, N_BRANCH * D), 0.02),
        'w_out': nrm((L, D, D), D ** -0.5),
        'norm1_post_g': gain((L, D)),
        'norm2_pre_g': gain((L, D)),
        'w_up': nrm((L, D, 2 * FFN_DIM), D ** -0.5),
        'ffn_dw_w': nrm((L, FFN_CONV_K, 2 * FFN_DIM), FFN_CONV_K ** -0.5),
        'ffn_dw_b': nrm((L, 2 * FFN_DIM), 0.02),
        'w_down': nrm((L, FFN_DIM, D), FFN_DIM ** -0.5),
        'norm2_post_g': gain((L, D)),
    }


def reference(x, mem, norm1_pre_g, w_in, conv_dw_w, conv_dw_b, conv_ln_g, conv_ln_b,
              conv_pw2, mem_norm_g, w_mem_kv, w_gate, b_gate, w_out, norm1_post_g,
              norm2_pre_g, w_up, ffn_dw_w, ffn_dw_b, w_down, norm2_post_g):
    b, s, d = x.shape
    split_at = [int(i) for i in np.cumsum(IN_SPLITS)[:-1]]
    for l in range(DEPTH):
        h = rms_norm(x, norm1_pre_g[l])
        proj = h @ w_in[l]
        conv_in, q, k, v, qi, wi, ki, qm = jnp.split(proj, split_at, axis=-1)
        y_conv = conformer_conv(conv_in, conv_dw_w[l], conv_dw_b[l],
                                conv_ln_g[l], conv_ln_b[l], conv_pw2[l])
        y_att = dsa_sparse_attention(
            q.reshape(b, s, ATT_HEADS, ATT_HEAD_DIM),
            k.reshape(b, s, ATT_HEADS, ATT_HEAD_DIM),
            v.reshape(b, s, ATT_HEADS, ATT_HEAD_DIM),
            qi.reshape(b, s, IDX_HEADS, IDX_HEAD_DIM), wi, ki)
        mkv = rms_norm(mem, mem_norm_g[l]) @ w_mem_kv[l]
        mk, mv = jnp.split(mkv, 2, axis=-1)
        m = mem.shape[1]
        y_mem = memory_attention(
            qm.reshape(b, s, MEM_HEADS, MEM_HEAD_DIM),
            mk.reshape(b, m, MEM_HEADS, MEM_HEAD_DIM),
            mv.reshape(b, m, MEM_HEADS, MEM_HEAD_DIM))
        g = jax.nn.sigmoid(h @ w_gate[l] + b_gate[l]).reshape(b, s, N_BRANCH, d)
        merged = g[:, :, 0] * y_conv + g[:, :, 1] * y_att + g[:, :, 2] * y_mem
        x = x + rms_norm(merged @ w_out[l], norm1_post_g[l])
        h2 = rms_norm(x, norm2_pre_g[l])
        u = causal_depthwise_conv(h2 @ w_up[l], ffn_dw_w[l], ffn_dw_b[l])
        u_gate, u_val = jnp.split(u, 2, axis=-1)
        x = x + rms_norm((jax.nn.silu(u_gate) * u_val) @ w_down[l], norm2_post_g[l])
    return x
```

```python
import functools

import jax
import jax.numpy as jnp
from jax import lax
from jax.experimental import pallas as pl
from jax.experimental.pallas import tpu as pltpu

EPS = 1e-6
D_MODEL = 1024
CONV_K = 31
ATT_HEADS = 8
ATT_HEAD_DIM = 128
IDX_HEADS = 8
IDX_HEAD_DIM = 64
TOPK_MAX = 256
MEM_HEADS = 4
MEM_HEAD_DIM = 256
FFN_DIM = 2816
FFN_CONV_K = 3

LANES = 128
SUBLANES = 8
VMEM_LIMIT_BYTES = 56 * 1024 * 1024

F32 = jnp.float32
BF16 = jnp.bfloat16
NEG_INF = float("-inf")


def _params(semantics):
    return pltpu.CompilerParams(dimension_semantics=semantics,
                                vmem_limit_bytes=VMEM_LIMIT_BYTES)


def _dot(a, b):
    return jnp.dot(a, b, preferred_element_type=F32)


def _dot_nt(a, b):
    return lax.dot_general(a, b, (((1,), (1,)), ((), ())), preferred_element_type=F32)


def _rms(x, g):
    return x * lax.rsqrt(jnp.mean(x * x, axis=-1, keepdims=True) + EPS) * g


def _split_bf16(v):
    hi = v.astype(BF16)
    lo = (v - hi.astype(F32)).astype(BF16)
    return hi, lo


def _proj_glu_kernel(x_ref, g_ref, wa_ref, wg_ref, o_ref, h_ref):
    @pl.when(pl.program_id(1) == 0)
    def _():
        h_ref[...] = _rms(x_ref[...], g_ref[...]).astype(BF16)

    h = h_ref[...]
    a = _dot(h, wa_ref[...])
    gate = _dot(h, wg_ref[...])
    o_ref[...] = a * jax.nn.sigmoid(gate)


def _proj_glu(x2, g, w_glu, tm, tn):
    t, d = x2.shape
    c = w_glu.shape[1] // 2
    nj = c // tn
    return pl.pallas_call(
        _proj_glu_kernel,
        grid=(t // tm, nj),
        in_specs=[
            pl.BlockSpec((tm, d), lambda i, j: (i, 0)),
            pl.BlockSpec((1, d), lambda i, j: (0, 0)),
            pl.BlockSpec((d, tn), lambda i, j: (0, j)),
            pl.BlockSpec((d, tn), lambda i, j: (0, j + nj)),
        ],
        out_specs=pl.BlockSpec((tm, tn), lambda i, j: (i, j)),
        out_shape=jax.ShapeDtypeStruct((t, c), F32),
        scratch_shapes=[pltpu.VMEM((tm, d), BF16)],
        compiler_params=_params(("arbitrary", "arbitrary")),
        name="proj_glu",
    )(x2, g, w_glu, w_glu)


def _proj_bf16_kernel(x_ref, g_ref, w_ref, o_ref, h_ref):
    @pl.when(pl.program_id(1) == 0)
    def _():
        h_ref[...] = _rms(x_ref[...], g_ref[...]).astype(BF16)

    o_ref[...] = _dot(h_ref[...], w_ref[...]).astype(BF16)


def _proj_bf16(x2, g, w, tm, tn):
    t, d = x2.shape
    n = w.shape[1]
    return pl.pallas_call(
        _proj_bf16_kernel,
        grid=(t // tm, n // tn),
        in_specs=[
            pl.BlockSpec((tm, d), lambda i, j: (i, 0)),
            pl.BlockSpec((1, d), lambda i, j: (0, 0)),
            pl.BlockSpec((d, tn), lambda i, j: (0, j)),
        ],
        out_specs=pl.BlockSpec((tm, tn), lambda i, j: (i, j)),
        out_shape=jax.ShapeDtypeStruct((t, n), BF16),
        scratch_shapes=[pltpu.VMEM((tm, d), BF16)],
        compiler_params=_params(("arbitrary", "arbitrary")),
        name="proj_bf16",
    )(x2, g, w)


def _proj_idx_kernel(x_ref, g_ref, whi_ref, wlo_ref, qcat_ref, kcat_ref, wi_ref):
    h = _rms(x_ref[...], g_ref[...])
    h_hi, h_lo = _split_bf16(h)
    whi = whi_ref[...]
    out = _dot(h_hi, whi) + _dot(h_lo, whi) + _dot(h_hi, wlo_ref[...])
    npair = IDX_HEADS // 2
    for p in range(npair):
        hi, lo = _split_bf16(out[:, p * LANES:(p + 1) * LANES])
        qcat_ref[:, 2 * p * LANES:(2 * p + 1) * LANES] = hi
        qcat_ref[:, (2 * p + 1) * LANES:(2 * p + 2) * LANES] = lo
    base = npair * LANES
    ka_hi, ka_lo = _split_bf16(out[:, base:base + LANES])
    kb_hi, kb_lo = _split_bf16(out[:, base + LANES:base + 2 * LANES])
    kcat_ref[:, 0 * LANES:1 * LANES] = ka_hi
    kcat_ref[:, 1 * LANES:2 * LANES] = ka_hi
    kcat_ref[:, 2 * LANES:3 * LANES] = kb_hi
    kcat_ref[:, 3 * LANES:4 * LANES] = kb_hi
    kcat_ref[:, 4 * LANES:5 * LANES] = ka_lo
    kcat_ref[:, 5 * LANES:6 * LANES] = kb_lo
    wi_ref[...] = out[:, base + 2 * LANES:base + 3 * LANES]


def _proj_idx(x2, g, w_hi, w_lo, tm):
    t, d = x2.shape
    n = w_hi.shape[1]
    npair = IDX_HEADS // 2
    return pl.pallas_call(
        _proj_idx_kernel,
        grid=(t // tm,),
        in_specs=[
            pl.BlockSpec((tm, d), lambda i: (i, 0)),
            pl.BlockSpec((1, d), lambda i: (0, 0)),
            pl.BlockSpec((d, n), lambda i: (0, 0)),
            pl.BlockSpec((d, n), lambda i: (0, 0)),
        ],
        out_specs=[
            pl.BlockSpec((tm, 2 * npair * LANES), lambda i: (i, 0)),
            pl.BlockSpec((tm, 6 * LANES), lambda i: (i, 0)),
            pl.BlockSpec((tm, LANES), lambda i: (i, 0)),
        ],
        out_shape=[
            jax.ShapeDtypeStruct((t, 2 * npair * LANES), BF16),
            jax.ShapeDtypeStruct((t, 6 * LANES), BF16),
            jax.ShapeDtypeStruct((t, LANES), F32),
        ],
        compiler_params=_params(("arbitrary",)),
        name="proj_idx",
    )(x2, g, w_hi, w_lo)


CONV_HALO = 32
CONV_ROWS = 32


def _conv_branch_kernel(y_ref, dww_ref, dwb_ref, lng_ref, lnb_ref, pw2_ref, o_ref,
                        ext_ref, cv_ref, *, ts, tiles_per_seq):
    c = y_ref.shape[1]
    first = pl.program_id(0) % tiles_per_seq == 0

    @pl.when(first)
    def _():
        ext_ref[0:CONV_HALO, :] = jnp.zeros((CONV_HALO, c), F32)

    @pl.when(jnp.logical_not(first))
    def _():
        ext_ref[0:CONV_HALO, :] = ext_ref[ts:ts + CONV_HALO, :]

    ext_ref[CONV_HALO:CONV_HALO + ts, :] = y_ref[...]

    off0 = CONV_HALO - (CONV_K - 1)
    for cg in range(c // LANES):
        cs = slice(cg * LANES, (cg + 1) * LANES)
        for rb in range(ts // CONV_ROWS):
            r0 = rb * CONV_ROWS
            acc = jnp.broadcast_to(dwb_ref[:, cs], (CONV_ROWS, LANES))
            for j in range(CONV_K):
                acc = acc + dww_ref[j:j + 1, cs] * ext_ref[r0 + off0 + j:r0 + off0 + j + CONV_ROWS, cs]
            cv_ref[r0:r0 + CONV_ROWS, cs] = acc

    y = cv_ref[...]
    mu = jnp.mean(y, axis=-1, keepdims=True)
    yc = y - mu
    var = jnp.mean(yc * yc, axis=-1, keepdims=True)
    z = yc * lax.rsqrt(var + EPS) * lng_ref[...] + lnb_ref[...]
    z = z * jax.nn.sigmoid(z)
    o_ref[...] = _dot(z.astype(BF16), pw2_ref[...])


def _conv_branch(yg, dw_w, dw_b, ln_g, ln_b, pw2, seq, ts):
    t, c = yg.shape
    kern = functools.partial(_conv_branch_kernel, ts=ts, tiles_per_seq=seq // ts)
    kp = dw_w.shape[0]
    return pl.pallas_call(
        kern,
        grid=(t // ts,),
        in_specs=[
            pl.BlockSpec((ts, c), lambda i: (i, 0)),
            pl.BlockSpec((kp, c), lambda i: (0, 0)),
            pl.BlockSpec((1, c), lambda i: (0, 0)),
            pl.BlockSpec((1, c), lambda i: (0, 0)),
            pl.BlockSpec((1, c), lambda i: (0, 0)),
            pl.BlockSpec((c, c), lambda i: (0, 0)),
        ],
        out_specs=pl.BlockSpec((ts, c), lambda i: (i, 0)),
        out_shape=jax.ShapeDtypeStruct((t, c), F32),
        scratch_shapes=[pltpu.VMEM((ts + CONV_HALO, c), F32), pltpu.VMEM((ts, c), F32)],
        compiler_params=_params(("arbitrary",)),
        name="conv_branch",
    )(yg, dw_w, dw_b, ln_g, ln_b, pw2)


DSA_TQ = 256
DSA_TK = 512


def _dsa_kernel(q_ref, k_ref, v_ref, qcat_ref, kcat_ref, wi_ref, tri_ref, o_ref,
                sc_ref, lg_ref, *, topk):
    tq, tk = DSA_TQ, DSA_TK
    qt = pl.program_id(1)
    nck = ((qt + 1) * tq + tk - 1) // tk
    idx_scale = (IDX_HEADS ** -0.5) * (IDX_HEAD_DIM ** -0.5)
    att_scale = ATT_HEAD_DIM ** -0.5

    q_pos = qt * tq + lax.broadcasted_iota(jnp.int32, (tq, 1), 0)
    lane_pos = lax.broadcasted_iota(jnp.int32, (1, tk), 1)

    def score_chunk(c, carry):
        r0 = pl.multiple_of(c * tk, tk)
        kc = kcat_ref[pl.ds(r0, tk), :]
        acc = jnp.zeros((tq, tk), F32)
        for h in range(IDX_HEADS):
            p, par = h // 2, h % 2
            qhl = qcat_ref[:, 2 * p * LANES:(2 * p + 2) * LANES]
            qh = qcat_ref[:, 2 * p * LANES:(2 * p + 1) * LANES]
            khh = kc[:, 2 * par * LANES:(2 * par + 2) * LANES]
            kl = kc[:, (4 + par) * LANES:(5 + par) * LANES]
            d = _dot_nt(qhl, khh) + _dot_nt(qh, kl)
            acc = acc + jnp.maximum(d, 0.0) * wi_ref[:, h:h + 1]
        s = acc * idx_scale
        causal = (c * tk + lane_pos) <= q_pos
        sc_ref[c] = jnp.where(causal, s, NEG_INF)
        return carry

    lax.fori_loop(0, nck, score_chunk, 0)

    def count_rows(pred_fn):
        def body(c, part):
            s = sc_ref[c]
            for g in range(tk // LANES):
                part = part + jnp.where(pred_fn(s[:, g * LANES:(g + 1) * LANES]), 1.0, 0.0)
            return part
        part = lax.fori_loop(0, nck, body, jnp.zeros((tq, LANES), F32))
        return jnp.sum(part, axis=1, keepdims=True)

    def key_to_float(u):
        bits = jnp.where(u < 0, u & jnp.int32(0x7FFFFFFF), ~u)
        return lax.bitcast_convert_type(bits, F32)

    def bisect(i, prefix):
        trial = prefix | lax.shift_left(jnp.int32(1), 31 - i)
        cand = key_to_float(trial)
        cnt = count_rows(lambda s: s >= cand)
        return jnp.where(cnt >= float(topk), trial, prefix)

    prefix = lax.fori_loop(0, 32, bisect, jnp.zeros((tq, 1), jnp.int32))
    few = (q_pos + 1) <= topk
    thr = jnp.where(few, NEG_INF, key_to_float(prefix))

    cnt_gt = count_rows(lambda s: s > thr)
    need = jnp.where(few, 0.0, float(topk) - cnt_gt)

    def bias_chunk(c, seen):
        s = sc_ref[c]
        eq = s == thr
        eqf = jnp.where(eq, 1.0, 0.0)
        before = _dot(eqf.astype(BF16), tri_ref[...]) + seen
        keep = (s > thr) | (eq & (before < need))
        sc_ref[c] = jnp.where(keep, 0.0, NEG_INF)
        return seen + jnp.sum(eqf, axis=1, keepdims=True)

    lax.fori_loop(0, nck, bias_chunk, jnp.zeros((tq, 1), F32))

    for h in range(ATT_HEADS):
        hs = slice(h * ATT_HEAD_DIM, (h + 1) * ATT_HEAD_DIM)
        qh = q_ref[:, hs]

        def logit_chunk(c, mpart):
            r0 = pl.multiple_of(c * tk, tk)
            l = _dot_nt(qh, k_ref[pl.ds(r0, tk), hs]) * att_scale + sc_ref[c]
            lg_ref[c] = l
            for g in range(tk // LANES):
                mpart = jnp.maximum(mpart, l[:, g * LANES:(g + 1) * LANES])
            return mpart

        mpart = lax.fori_loop(0, nck, logit_chunk, jnp.full((tq, LANES), NEG_INF, F32))
        m = jnp.max(mpart, axis=1, keepdims=True)

        def pv_chunk(c, carry):
            lpart, acc = carry
            r0 = pl.multiple_of(c * tk, tk)
            p = jnp.exp(lg_ref[c] - m)
            for g in range(tk // LANES):
                lpart = lpart + p[:, g * LANES:(g + 1) * LANES]
            acc = acc + _dot(p.astype(BF16), v_ref[pl.ds(r0, tk), hs])
            return lpart, acc

        lpart, acc = lax.fori_loop(
            0, nck, pv_chunk,
            (jnp.zeros((tq, LANES), F32), jnp.zeros((tq, ATT_HEAD_DIM), F32)))
        o_ref[:, hs] = acc / jnp.sum(lpart, axis=1, keepdims=True)


def _dsa(qkvm, qcat, kcat, wi, tri, batch, seq):
    tq, tk = DSA_TQ, DSA_TK
    d = D_MODEL
    nq = seq // tq
    topk = min(TOPK_MAX, seq // 4)
    kern = functools.partial(_dsa_kernel, topk=topk)
    return pl.pallas_call(
        kern,
        grid=(batch, nq),
        in_specs=[
            pl.BlockSpec((tq, d), lambda b, i: (b * nq + i, 0)),
            pl.BlockSpec((seq, d), lambda b, i: (b, 1), pipeline_mode=pl.Buffered(1)),
            pl.BlockSpec((seq, d), lambda b, i: (b, 2), pipeline_mode=pl.Buffered(1)),
            pl.BlockSpec((tq, qcat.shape[1]), lambda b, i: (b * nq + i, 0)),
            pl.BlockSpec((seq, kcat.shape[1]), lambda b, i: (b, 0)),
            pl.BlockSpec((tq, LANES), lambda b, i: (b * nq + i, 0)),
            pl.BlockSpec((tk, tk), lambda b, i: (0, 0)),
        ],
        out_specs=pl.BlockSpec((tq, d), lambda b, i: (b * nq + i, 0)),
        out_shape=jax.ShapeDtypeStruct((batch * seq, d), F32),
        scratch_shapes=[pltpu.VMEM((seq // tk, tq, tk), F32),
                        pltpu.VMEM((seq // tk, tq, tk), F32)],
        compiler_params=_params(("arbitrary", "arbitrary")),
        name="dsa",
    )(qkvm, qkvm, qkvm, qcat, kcat, wi, tri)


def _mem_attn_kernel(qm_ref, mk_ref, mv_ref, o_ref):
    scale = MEM_HEAD_DIM ** -0.5
    for h in range(MEM_HEADS):
        hs = slice(h * MEM_HEAD_DIM, (h + 1) * MEM_HEAD_DIM)
        l = _dot_nt(qm_ref[:, hs], mk_ref[:, hs]) * scale
        p = jnp.exp(l - jnp.max(l, axis=1, keepdims=True))
        acc = _dot(p.astype(BF16), mv_ref[:, hs])
        o_ref[:, hs] = acc / jnp.sum(p, axis=1, keepdims=True)


def _mem_attn(qkvm, mkv, batch, seq, n_mem, tm):
    d = D_MODEL
    nt = seq // tm
    return pl.pallas_call(
        _mem_attn_kernel,
        grid=(batch, nt),
        in_specs=[
            pl.BlockSpec((tm, d), lambda b, i: (b * nt + i, 3)),
            pl.BlockSpec((n_mem, d), lambda b, i: (b, 0)),
            pl.BlockSpec((n_mem, d), lambda b, i: (b, 1)),
        ],
        out_specs=pl.BlockSpec((tm, d), lambda b, i: (b * nt + i, 0)),
        out_shape=jax.ShapeDtypeStruct((batch * seq, d), F32),
        compiler_params=_params(("arbitrary", "arbitrary")),
        name="mem_attn",
    )(qkvm, mkv, mkv)


def _merge_kernel(x_ref, yc_ref, ya_ref, ym_ref, g1_ref, wg_ref, bg_ref, wo_ref, gp_ref, o_ref):
    d = x_ref.shape[1]
    x = x_ref[...]
    h = _rms(x, g1_ref[...]).astype(BF16)
    merged = jnp.zeros(x.shape, F32)
    for br, y_ref in enumerate((yc_ref, ya_ref, ym_ref)):
        cs = slice(br * d, (br + 1) * d)
        gate = jax.nn.sigmoid(_dot(h, wg_ref[:, cs]) + bg_ref[:, cs])
        merged = merged + gate * y_ref[...]
    out = _dot(merged.astype(BF16), wo_ref[...])
    o_ref[...] = x + _rms(out, gp_ref[...])


def _merge(x2, yc, ya, ym, g1, w_gate, b_gate, w_out, g_post, tm):
    t, d = x2.shape
    row = lambda i: (i, 0)
    fix = lambda i: (0, 0)
    return pl.pallas_call(
        _merge_kernel,
        grid=(t // tm,),
        in_specs=[
            pl.BlockSpec((tm, d), row), pl.BlockSpec((tm, d), row),
            pl.BlockSpec((tm, d), row), pl.BlockSpec((tm, d), row),
            pl.BlockSpec((1, d), fix),
            pl.BlockSpec((d, 3 * d), fix),
            pl.BlockSpec((1, 3 * d), fix),
            pl.BlockSpec((d, d), fix),
            pl.BlockSpec((1, d), fix),
        ],
        out_specs=pl.BlockSpec((tm, d), row),
        out_shape=jax.ShapeDtypeStruct((t, d), F32),
        compiler_params=_params(("arbitrary",)),
        name="merge_out",
    )(x2, yc, ya, ym, g1, w_gate, b_gate, w_out, g_post)


FFN_FC = 256


def _ffn_kernel(x_ref, g2_ref, wug_ref, wuv_ref, cwg_ref, cwv_ref, cbg_ref, cbv_ref,
                wd_ref, gp_ref, o_ref, h_ref, acc_ref, ext_ref, tail_ref, *, tm, tiles_per_seq):
    i = pl.program_id(0)
    j = pl.program_id(1)
    nj = pl.num_programs(1)
    first = i % tiles_per_seq == 0
    halo = SUBLANES

    @pl.when(j == 0)
    def _():
        h_ref[...] = _rms(x_ref[...], g2_ref[...]).astype(BF16)
        acc_ref[...] = jnp.zeros(acc_ref.shape, F32)

    h = h_ref[...]

    def conv_half(w_ref, cw_ref, cb_ref, slot):
        up = _dot(h, w_ref[...])

        @pl.when(first)
        def _():
            ext_ref[slot, 0:halo, :] = jnp.zeros((halo, up.shape[1]), F32)

        @pl.when(jnp.logical_not(first))
        def _():
            ext_ref[slot, 0:halo, :] = tail_ref[2 * j + slot]

        ext_ref[slot, halo:halo + tm, :] = up
        tail_ref[2 * j + slot] = up[tm - halo:tm, :]
        u = cb_ref[...] + cw_ref[2:3, :] * up
        u = u + cw_ref[1:2, :] * ext_ref[slot, halo - 1:halo - 1 + tm, :]
        u = u + cw_ref[0:1, :] * ext_ref[slot, halo - 2:halo - 2 + tm, :]
        return u

    ug = conv_half(wug_ref, cwg_ref, cbg_ref, 0)
    uv = conv_half(wuv_ref, cwv_ref, cbv_ref, 1)
    act = (ug * jax.nn.sigmoid(ug)) * uv
    acc_ref[...] += _dot(act.astype(BF16), wd_ref[...])

    @pl.when(j == nj - 1)
    def _():
        o_ref[...] = x_ref[...] + _rms(acc_ref[...], gp_ref[...])


def _ffn(x1, g2, w_up, cw, cb, w_down, g_post, seq, tm):
    t, d = x1.shape
    f = w_down.shape[0]
    fc = FFN_FC
    nj = f // fc
    kern = functools.partial(_ffn_kernel, tm=tm, tiles_per_seq=seq // tm)
    kp = cw.shape[0]
    return pl.pallas_call(
        kern,
        grid=(t // tm, nj),
        in_specs=[
            pl.BlockSpec((tm, d), lambda i, j: (i, 0)),
            pl.BlockSpec((1, d), lambda i, j: (0, 0)),
            pl.BlockSpec((d, fc), lambda i, j: (0, j)),
            pl.BlockSpec((d, fc), lambda i, j: (0, j + nj)),
            pl.BlockSpec((kp, fc), lambda i, j: (0, j)),
            pl.BlockSpec((kp, fc), lambda i, j: (0, j + nj)),
            pl.BlockSpec((1, fc), lambda i, j: (0, j)),
            pl.BlockSpec((1, fc), lambda i, j: (0, j + nj)),
            pl.BlockSpec((fc, d), lambda i, j: (j, 0)),
            pl.BlockSpec((1, d), lambda i, j: (0, 0)),
        ],
        out_specs=pl.BlockSpec((tm, d), lambda i, j: (i, 0)),
        out_shape=jax.ShapeDtypeStruct((t, d), F32),
        scratch_shapes=[
            pltpu.VMEM((tm, d), BF16),
            pltpu.VMEM((tm, d), F32),
            pltpu.VMEM((2, tm + SUBLANES, fc), F32),
            pltpu.VMEM((2 * nj, SUBLANES, fc), F32),
        ],
        compiler_params=_params(("arbitrary", "arbitrary")),
        name="ffn",
    )(x1, g2, w_up, w_up, cw, cw, cb, cb, w_down, g_post)


def _pad_rows(a, rows):
    return jnp.pad(a, ((0, rows - a.shape[0]), (0, 0)))


def _layer(x2, mem2, batch, seq, n_mem, p):
    d = D_MODEL
    row = lambda v: v.reshape(1, -1)
    w_in = p["w_in"]
    c2 = 2 * d
    o_q, o_qi = c2, c2 + 3 * d
    o_wi = o_qi + IDX_HEADS * IDX_HEAD_DIM
    o_ki = o_wi + IDX_HEADS
    o_qm = o_ki + IDX_HEAD_DIM

    w_glu = w_in[:, :c2].astype(BF16)
    w_qkvm = jnp.concatenate([w_in[:, o_q:o_qi], w_in[:, o_qm:]], axis=1).astype(BF16)
    w_ki = w_in[:, o_ki:o_qm]
    zk = jnp.zeros((d, LANES - IDX_HEAD_DIM), F32)
    w_idx = jnp.concatenate(
        [w_in[:, o_qi:o_wi], w_ki, zk, zk, w_ki,
         w_in[:, o_wi:o_ki], jnp.zeros((d, LANES - IDX_HEADS), F32)], axis=1)
    w_idx_hi = w_idx.astype(BF16)
    w_idx_lo = (w_idx - w_idx_hi.astype(F32)).astype(BF16)

    g1 = row(p["norm1_pre_g"])
    yg = _proj_glu(x2, g1, w_glu, tm=512, tn=512)
    qkvm = _proj_bf16(x2, g1, w_qkvm, tm=512, tn=1024)
    qcat, kcat, wi = _proj_idx(x2, g1, w_idx_hi, w_idx_lo, tm=512)

    dw_w = _pad_rows(p["conv_dw_w"], 32)
    y_conv = _conv_branch(yg, dw_w, row(p["conv_dw_b"]), row(p["conv_ln_g"]),
                          row(p["conv_ln_b"]), p["conv_pw2"].astype(BF16), seq, ts=256)

    tk = DSA_TK
    tri = (lax.broadcasted_iota(jnp.int32, (tk, tk), 0)
           < lax.broadcasted_iota(jnp.int32, (tk, tk), 1)).astype(BF16)
    y_att = _dsa(qkvm, qcat, kcat, wi, tri, batch, seq)

    mkv = _proj_bf16(mem2, row(p["mem_norm_g"]), p["w_mem_kv"].astype(BF16), tm=n_mem, tn=1024)
    y_mem = _mem_attn(qkvm, mkv, batch, seq, n_mem, tm=512)

    x1 = _merge(x2, y_conv, y_att, y_mem, g1, p["w_gate"].astype(BF16), row(p["b_gate"]),
                p["w_out"].astype(BF16), row(p["norm1_post_g"]), tm=256)

    x2o = _ffn(x1, row(p["norm2_pre_g"]), p["w_up"].astype(BF16),
               _pad_rows(p["ffn_dw_w"], SUBLANES), row(p["ffn_dw_b"]),
               p["w_down"].astype(BF16), row(p["norm2_post_g"]), seq, tm=512)
    return x2o


def kernel(x, mem, norm1_pre_g, w_in, conv_dw_w, conv_dw_b, conv_ln_g, conv_ln_b, conv_pw2,
           mem_norm_g, w_mem_kv, w_gate, b_gate, w_out, norm1_post_g, norm2_pre_g, w_up,
           ffn_dw_w, ffn_dw_b, w_down, norm2_post_g):
    batch, seq, d = x.shape
    n_mem = mem.shape[1]
    assert d == D_MODEL and seq % DSA_TK == 0 and n_mem % SUBLANES == 0
    names = ("norm1_pre_g", "w_in", "conv_dw_w", "conv_dw_b", "conv_ln_g", "conv_ln_b",
             "conv_pw2", "mem_norm_g", "w_mem_kv", "w_gate", "b_gate", "w_out", "norm1_post_g",
             "norm2_pre_g", "w_up", "ffn_dw_w", "ffn_dw_b", "w_down", "norm2_post_g")
    vals = (norm1_pre_g, w_in, conv_dw_w, conv_dw_b, conv_ln_g, conv_ln_b, conv_pw2,
            mem_norm_g, w_mem_kv, w_gate, b_gate, w_out, norm1_post_g, norm2_pre_g, w_up,
            ffn_dw_w, ffn_dw_b, w_down, norm2_post_g)
    x2 = x.reshape(batch * seq, d)
    mem2 = mem.reshape(batch * n_mem, d)
    for l in range(norm1_pre_g.shape[0]):
        x2 = _layer(x2, mem2, batch, seq, n_mem, {n: v[l] for n, v in zip(names, vals)})
    return x2.reshape(batch, seq, d)
```

```python
import functools

import jax
import jax.numpy as jnp
from jax import lax
from jax.experimental import pallas as pl
from jax.experimental.pallas import tpu as pltpu

EPS = 1e-6
D_MODEL = 1024
CONV_K = 31
ATT_HEADS = 8
ATT_HEAD_DIM = 128
IDX_HEADS = 8
IDX_HEAD_DIM = 64
TOPK_MAX = 256
MEM_HEADS = 4
MEM_HEAD_DIM = 256
FFN_DIM = 2816
FFN_CONV_K = 3

LANES = 128
SUBLANES = 8
VMEM_LIMIT_BYTES = 56 * 1024 * 1024

F32 = jnp.float32
BF16 = jnp.bfloat16
NEG_INF = float("-inf")
POS_INF = float("inf")


def _params(semantics):
    return pltpu.CompilerParams(dimension_semantics=semantics,
                                vmem_limit_bytes=VMEM_LIMIT_BYTES)


def _resident(shape):
    return pl.BlockSpec(shape, lambda *_: (0,) * len(shape), pipeline_mode=pl.Buffered(1))


def _dot(a, b):
    return jnp.dot(a, b, preferred_element_type=F32)


def _dot_nt(a, b):
    return lax.dot_general(a, b, (((1,), (1,)), ((), ())), preferred_element_type=F32)


def _rms(x, g):
    return x * lax.rsqrt(jnp.mean(x * x, axis=-1, keepdims=True) + EPS) * g


def _split_bf16(v):
    hi = v.astype(BF16).astype(F32)
    lo = (v - hi).astype(BF16).astype(F32)
    return hi, lo


PROJ_TN = 512
IDX_CAT = 2 * LANES


def _proj_kernel(x_ref, g_ref, wglu_ref, wqkvm_ref, whi_ref, wlo_ref,
                 yg_ref, qkvm_ref, qcat_ref, kcat_ref, wi_ref):
    h = _rms(x_ref[...], g_ref[...])
    h_hi = h.astype(BF16)
    h_lo = (h - h_hi.astype(F32)).astype(BF16)

    c = yg_ref.shape[1]
    for j in range(c // PROJ_TN):
        a = _dot(h_hi, wglu_ref[:, j * PROJ_TN:(j + 1) * PROJ_TN])
        gate = _dot(h_hi, wglu_ref[:, c + j * PROJ_TN:c + (j + 1) * PROJ_TN])
        yg_ref[:, j * PROJ_TN:(j + 1) * PROJ_TN] = a * jax.nn.sigmoid(gate)

    for j in range(qkvm_ref.shape[1] // PROJ_TN):
        cs = slice(j * PROJ_TN, (j + 1) * PROJ_TN)
        qkvm_ref[:, cs] = _dot(h_hi, wqkvm_ref[:, cs]).astype(BF16)

    whi = whi_ref[...]
    out = _dot(h_hi, whi) + _dot(h_lo, whi) + _dot(h_hi, wlo_ref[...])
    low_half = lax.broadcasted_iota(jnp.int32, (1, LANES), 1) < IDX_HEAD_DIM
    npair = IDX_HEADS // 2
    for p in range(npair):
        hi, lo = _split_bf16(out[:, p * LANES:(p + 1) * LANES])
        hi_sw = pltpu.roll(hi, IDX_HEAD_DIM, 1)
        lo_sw = pltpu.roll(lo, IDX_HEAD_DIM, 1)
        base = 2 * p * IDX_CAT
        qcat_ref[:, base:base + LANES] = jnp.where(low_half, hi, lo_sw).astype(BF16)
        qcat_ref[:, base + LANES:base + 2 * LANES] = jnp.where(low_half, hi, 0.0).astype(BF16)
        qcat_ref[:, base + 2 * LANES:base + 3 * LANES] = jnp.where(low_half, hi_sw, lo).astype(BF16)
        qcat_ref[:, base + 3 * LANES:base + 4 * LANES] = jnp.where(low_half, hi_sw, 0.0).astype(BF16)
    kb = npair * LANES
    k_hi, k_lo = _split_bf16(out[:, kb:kb + LANES])
    kcat_ref[:, 0:LANES] = (k_hi + pltpu.roll(k_hi, IDX_HEAD_DIM, 1)).astype(BF16)
    kcat_ref[:, LANES:2 * LANES] = k_lo.astype(BF16)
    wi_ref[...] = out[:, kb + LANES:kb + 2 * LANES]


def _proj(x2, g, w_glu, w_qkvm, w_hi, w_lo, tm):
    t, d = x2.shape
    c = w_glu.shape[1] // 2
    nq = w_qkvm.shape[1]
    row = lambda i: (i, 0)
    return pl.pallas_call(
        _proj_kernel,
        grid=(t // tm,),
        in_specs=[
            pl.BlockSpec((tm, d), row),
            _resident((1, d)),
            _resident(w_glu.shape),
            _resident(w_qkvm.shape),
            _resident(w_hi.shape),
            _resident(w_lo.shape),
        ],
        out_specs=[
            pl.BlockSpec((tm, c), row),
            pl.BlockSpec((tm, nq), row),
            pl.BlockSpec((tm, IDX_HEADS * IDX_CAT), row),
            pl.BlockSpec((tm, IDX_CAT), row),
            pl.BlockSpec((tm, LANES), row),
        ],
        out_shape=[
            jax.ShapeDtypeStruct((t, c), F32),
            jax.ShapeDtypeStruct((t, nq), BF16),
            jax.ShapeDtypeStruct((t, IDX_HEADS * IDX_CAT), BF16),
            jax.ShapeDtypeStruct((t, IDX_CAT), BF16),
            jax.ShapeDtypeStruct((t, LANES), F32),
        ],
        compiler_params=_params(("arbitrary",)),
        name="proj",
    )(x2, g, w_glu, w_qkvm, w_hi, w_lo)


def _proj_bf16_kernel(x_ref, g_ref, w_ref, o_ref):
    h = _rms(x_ref[...], g_ref[...]).astype(BF16)
    o_ref[...] = _dot(h, w_ref[...]).astype(BF16)


def _proj_bf16(x2, g, w, tm):
    t, d = x2.shape
    n = w.shape[1]
    return pl.pallas_call(
        _proj_bf16_kernel,
        grid=(t // tm,),
        in_specs=[
            pl.BlockSpec((tm, d), lambda i: (i, 0)),
            _resident((1, d)),
            _resident(w.shape),
        ],
        out_specs=pl.BlockSpec((tm, n), lambda i: (i, 0)),
        out_shape=jax.ShapeDtypeStruct((t, n), BF16),
        compiler_params=_params(("arbitrary",)),
        name="proj_mem",
    )(x2, g, w)


CONV_HALO = 32
CONV_ROWS = 64


def _conv_branch_kernel(y_ref, dww_ref, dwb_ref, lng_ref, lnb_ref, pw2_ref, o_ref,
                        ext_ref, sh_ref, cv_ref, *, ts, tiles_per_seq):
    c = y_ref.shape[1]
    first = pl.program_id(0) % tiles_per_seq == 0

    @pl.when(first)
    def _():
        ext_ref[0:CONV_HALO, :] = jnp.zeros((CONV_HALO, c), F32)

    @pl.when(jnp.logical_not(first))
    def _():
        ext_ref[0:CONV_HALO, :] = ext_ref[ts:ts + CONV_HALO, :]

    ext_ref[CONV_HALO:CONV_HALO + ts, :] = y_ref[...]

    off0 = CONV_HALO - (CONV_K - 1)
    span = ts + CONV_HALO - SUBLANES
    for ph in range(1, SUBLANES):
        sh_ref[ph - 1, 0:span, :] = ext_ref[ph:ph + span, :]
    for cg in range(c // LANES):
        cs = slice(cg * LANES, (cg + 1) * LANES)
        for rb in range(ts // CONV_ROWS):
            r0 = rb * CONV_ROWS
            acc = jnp.broadcast_to(dwb_ref[:, cs], (CONV_ROWS, LANES))
            for j in range(CONV_K):
                ph = (off0 + j) % SUBLANES
                a0 = r0 + off0 + j - ph
                if ph == 0:
                    tap = ext_ref[a0:a0 + CONV_ROWS, cs]
                else:
                    tap = sh_ref[ph - 1, a0:a0 + CONV_ROWS, cs]
                acc = acc + dww_ref[j:j + 1, cs] * tap
            cv_ref[r0:r0 + CONV_ROWS, cs] = acc

    y = cv_ref[...]
    mu = jnp.mean(y, axis=-1, keepdims=True)
    yc = y - mu
    var = jnp.mean(yc * yc, axis=-1, keepdims=True)
    z = yc * lax.rsqrt(var + EPS) * lng_ref[...] + lnb_ref[...]
    z = z * jax.nn.sigmoid(z)
    o_ref[...] = _dot(z.astype(BF16), pw2_ref[...])


def _conv_branch(yg, dw_w, dw_b, ln_g, ln_b, pw2, seq, ts):
    t, c = yg.shape
    kern = functools.partial(_conv_branch_kernel, ts=ts, tiles_per_seq=seq // ts)
    return pl.pallas_call(
        kern,
        grid=(t // ts,),
        in_specs=[
            pl.BlockSpec((ts, c), lambda i: (i, 0)),
            _resident(dw_w.shape),
            _resident((1, c)),
            _resident((1, c)),
            _resident((1, c)),
            _resident((c, c)),
        ],
        out_specs=pl.BlockSpec((ts, c), lambda i: (i, 0)),
        out_shape=jax.ShapeDtypeStruct((t, c), F32),
        scratch_shapes=[pltpu.VMEM((ts + CONV_HALO, c), F32),
                        pltpu.VMEM((SUBLANES - 1, ts + CONV_HALO, c), F32),
                        pltpu.VMEM((ts, c), F32)],
        compiler_params=_params(("arbitrary",)),
        name="conv_branch",
    )(yg, dw_w, dw_b, ln_g, ln_b, pw2)


DSA_TQ = 256
DSA_TK = 512
DSA_NACC = 4
DSA_HG = 4
BISECT_LINEAR = 24
BISECT_MAX = 96


def _ordered_key(f):
    b = lax.bitcast_convert_type(f, jnp.int32)
    return b ^ ((b >> 31) & jnp.int32(0x7FFFFFFF))


def _from_ordered_key(k):
    return lax.bitcast_convert_type(k ^ ((k >> 31) & jnp.int32(0x7FFFFFFF)), F32)


def _dsa_kernel(q_ref, k_ref, vt_ref, qcat_ref, kcat_ref, wit_ref, tri_ref, o_ref,
                sc_ref, lg_ref, acc_ref, *, topk):
    tq, tk = DSA_TQ, DSA_TK
    qt = pl.program_id(1)
    nck = ((qt + 1) * tq + tk - 1) // tk
    idx_scale = (IDX_HEADS ** -0.5) * (IDX_HEAD_DIM ** -0.5)
    att_scale = ATT_HEAD_DIM ** -0.5
    kf = float(topk)

    q_pos = qt * tq + lax.broadcasted_iota(jnp.int32, (1, tq), 1)
    key_off = lax.broadcasted_iota(jnp.int32, (tk, 1), 0)

    def score_chunk(c, carry):
        mn, mx = carry
        r0 = pl.multiple_of(c * tk, tk)
        kc = kcat_ref[pl.ds(r0, tk), :]
        acc = jnp.zeros((tk, tq), F32)
        for h in range(IDX_HEADS):
            d = _dot_nt(kc, qcat_ref[:, h * IDX_CAT:(h + 1) * IDX_CAT])
            acc = acc + jnp.maximum(d, 0.0) * wit_ref[h:h + 1, :]
        s = acc * idx_scale
        causal = (c * tk + key_off) <= q_pos
        lo_s = jnp.where(causal, s, POS_INF)
        hi_s = jnp.where(causal, s, NEG_INF)
        sc_ref[c] = hi_s
        mn = jnp.minimum(mn, jnp.min(lo_s, axis=0, keepdims=True))
        mx = jnp.maximum(mx, jnp.max(hi_s, axis=0, keepdims=True))
        return mn, mx

    row_min, row_max = lax.fori_loop(
        0, nck, score_chunk,
        (jnp.full((1, tq), POS_INF, F32), jnp.full((1, tq), NEG_INF, F32)))

    def count_rows(cand, strict):
        cb = jnp.broadcast_to(cand, (SUBLANES, tq))

        def body(c, parts):
            parts = list(parts)
            for i in range(tk // SUBLANES):
                s = sc_ref[c, i * SUBLANES:(i + 1) * SUBLANES, :]
                hit = (s > cb) if strict else (s >= cb)
                a = i % DSA_NACC
                parts[a] = jnp.where(hit, parts[a] + 1.0, parts[a])
            return tuple(parts)

        parts = lax.fori_loop(0, nck, body,
                              tuple(jnp.zeros((SUBLANES, tq), F32) for _ in range(DSA_NACC)))
        return jnp.sum(functools.reduce(jnp.add, parts), axis=0, keepdims=True)

    n_valid = (q_pos + 1).astype(F32)
    few = n_valid <= kf
    lo0 = row_min
    hi0 = _from_ordered_key(_ordered_key(row_max) + 1)

    def not_done(state):
        it, active = state[0], state[1]
        return jnp.logical_and(active > 0, it < BISECT_MAX)

    def bisect(state):
        it, _, lo, hi, cnt_lo, done = state
        klo, khi = _ordered_key(lo), _ordered_key(hi)
        mid_bits = _from_ordered_key((klo >> 1) + (khi >> 1) + (klo & khi & 1))
        mid = jnp.where(it < BISECT_LINEAR, 0.5 * lo + 0.5 * hi, mid_bits)
        cnt = count_rows(mid, strict=False)
        ge = cnt >= kf
        upd = done < 0.5
        lo = jnp.where(upd & ge, mid, lo)
        cnt_lo = jnp.where(upd & ge, cnt, cnt_lo)
        hi = jnp.where(upd & jnp.logical_not(ge), mid, hi)
        adjacent = _ordered_key(hi) <= _ordered_key(lo) + 1
        done = jnp.where((cnt_lo == kf) | adjacent, 1.0, done)
        active = jnp.max(1.0 - done).astype(jnp.int32)
        return it + 1, active, lo, hi, cnt_lo, done

    done0 = jnp.where(few | (n_valid == kf), 1.0, 0.0)
    active0 = jnp.max(1.0 - done0).astype(jnp.int32)
    state = lax.while_loop(not_done, bisect,
                           (jnp.int32(0), active0, lo0, hi0, n_valid, done0))
    thr = jnp.where(few, NEG_INF, state[2])

    cnt_gt = count_rows(thr, strict=True)
    need = jnp.where(few, 0.0, kf - cnt_gt)

    def bias_chunk(c, seen):
        s = sc_ref[c]
        eq = s == thr
        eqf = jnp.where(eq, 1.0, 0.0)
        before = _dot(tri_ref[...], eqf.astype(BF16)) + seen
        keep = (s > thr) | (eq & (before < need))
        sc_ref[c] = jnp.where(keep, 0.0, NEG_INF)
        return seen + jnp.sum(eqf, axis=0, keepdims=True)

    lax.fori_loop(0, nck, bias_chunk, jnp.zeros((1, tq), F32))

    def fold_rows(x, op, parts):
        parts = list(parts)
        for i in range(tk // SUBLANES):
            a = i % DSA_NACC
            parts[a] = op(parts[a], x[i * SUBLANES:(i + 1) * SUBLANES, :])
        return tuple(parts)

    def init_parts(value):
        return tuple(jnp.full((SUBLANES, tq), value, F32) for _ in range(DSA_NACC))

    for grp in range(ATT_HEADS // DSA_HG):
        heads = [grp * DSA_HG + i for i in range(DSA_HG)]

        def logit_chunk(c, carry):
            r0 = pl.multiple_of(c * tk, tk)
            out = []
            for i, h in enumerate(heads):
                hs = slice(h * ATT_HEAD_DIM, (h + 1) * ATT_HEAD_DIM)
                l = _dot_nt(k_ref[pl.ds(r0, tk), hs], q_ref[:, hs]) * att_scale + sc_ref[c]
                lg_ref[i, c] = l
                out.append(fold_rows(l, jnp.maximum, carry[i]))
            return tuple(out)

        mparts = lax.fori_loop(0, nck, logit_chunk,
                               tuple(init_parts(NEG_INF) for _ in range(DSA_HG)))
        m_rows = [jnp.max(functools.reduce(jnp.maximum, mp), axis=0, keepdims=True)
                  for mp in mparts]
        acc_ref[...] = jnp.zeros(acc_ref.shape, F32)

        def pv_chunk(c, carry):
            out = []
            for i, h in enumerate(heads):
                hs = slice(h * ATT_HEAD_DIM, (h + 1) * ATT_HEAD_DIM)
                p = jnp.exp(lg_ref[i, c] - m_rows[i])
                out.append(fold_rows(p, jnp.add, carry[i]))
                acc_ref[i] = acc_ref[i] + _dot(vt_ref[c, hs, :], p.astype(BF16))
            return tuple(out)

        lparts = lax.fori_loop(0, nck, pv_chunk, tuple(init_parts(0.0) for _ in range(DSA_HG)))
        for i, h in enumerate(heads):
            hs = slice(h * ATT_HEAD_DIM, (h + 1) * ATT_HEAD_DIM)
            lsum = jnp.sum(functools.reduce(jnp.add, lparts[i]), axis=0, keepdims=True)
            o_ref[:, hs] = (acc_ref[i] / lsum).T


def _dsa(qkvm, vt, qcat, kcat, wit, tri, batch, seq):
    tq, tk = DSA_TQ, DSA_TK
    d = D_MODEL
    nq = seq // tq
    nc = seq // tk
    topk = min(TOPK_MAX, seq // 4)
    kern = functools.partial(_dsa_kernel, topk=topk)
    qrow = lambda b, i: (b * nq + i, 0)
    return pl.pallas_call(
        kern,
        grid=(batch, nq),
        in_specs=[
            pl.BlockSpec((tq, d), qrow),
            pl.BlockSpec((seq, d), lambda b, i: (b, 1), pipeline_mode=pl.Buffered(1)),
            pl.BlockSpec((nc, d, tk), lambda b, i: (b, 0, 0), pipeline_mode=pl.Buffered(1)),
            pl.BlockSpec((tq, qcat.shape[1]), qrow),
            pl.BlockSpec((seq, kcat.shape[1]), lambda b, i: (b, 0)),
            pl.BlockSpec((IDX_HEADS, tq), lambda b, i: (0, b * nq + i)),
            _resident((tk, tk)),
        ],
        out_specs=pl.BlockSpec((tq, d), qrow),
        out_shape=jax.ShapeDtypeStruct((batch * seq, d), F32),
        scratch_shapes=[
            pltpu.VMEM((nc, tk, tq), F32),
            pltpu.VMEM((DSA_HG, nc, tk, tq), F32),
            pltpu.VMEM((DSA_HG, ATT_HEAD_DIM, tq), F32),
        ],
        compiler_params=_params(("arbitrary", "arbitrary")),
        name="dsa",
    )(qkvm, qkvm, vt, qcat, kcat, wit, tri)


def _mem_attn_kernel(qm_ref, mk_ref, mv_ref, o_ref):
    scale = MEM_HEAD_DIM ** -0.5
    for h in range(MEM_HEADS):
        hs = slice(h * MEM_HEAD_DIM, (h + 1) * MEM_HEAD_DIM)
        l = _dot_nt(qm_ref[:, hs], mk_ref[:, hs]) * scale
        p = jnp.exp(l - jnp.max(l, axis=1, keepdims=True))
        acc = _dot(p.astype(BF16), mv_ref[:, hs])
        o_ref[:, hs] = acc / jnp.sum(p, axis=1, keepdims=True)


def _mem_attn(qkvm, mkv, batch, seq, n_mem, tm):
    d = D_MODEL
    nt = seq // tm
    return pl.pallas_call(
        _mem_attn_kernel,
        grid=(batch, nt),
        in_specs=[
            pl.BlockSpec((tm, d), lambda b, i: (b * nt + i, 3)),
            pl.BlockSpec((n_mem, d), lambda b, i: (b, 0)),
            pl.BlockSpec((n_mem, d), lambda b, i: (b, 1)),
        ],
        out_specs=pl.BlockSpec((tm, d), lambda b, i: (b * nt + i, 0)),
        out_shape=jax.ShapeDtypeStruct((batch * seq, d), F32),
        compiler_params=_params(("arbitrary", "arbitrary")),
        name="mem_attn",
    )(qkvm, mkv, mkv)


def _merge_kernel(x_ref, yc_ref, ya_ref, ym_ref, g1_ref, wg_ref, bg_ref, wo_ref, gp_ref, o_ref):
    d = x_ref.shape[1]
    x = x_ref[...]
    h = _rms(x, g1_ref[...]).astype(BF16)
    merged = jnp.zeros(x.shape, F32)
    for br, y_ref in enumerate((yc_ref, ya_ref, ym_ref)):
        cs = slice(br * d, (br + 1) * d)
        gate = jax.nn.sigmoid(_dot(h, wg_ref[:, cs]) + bg_ref[:, cs])
        merged = merged + gate * y_ref[...]
    out = _dot(merged.astype(BF16), wo_ref[...])
    o_ref[...] = x + _rms(out, gp_ref[...])


def _merge(x2, yc, ya, ym, g1, w_gate, b_gate, w_out, g_post, tm):
    t, d = x2.shape
    row = lambda i: (i, 0)
    return pl.pallas_call(
        _merge_kernel,
        grid=(t // tm,),
        in_specs=[
            pl.BlockSpec((tm, d), row), pl.BlockSpec((tm, d), row),
            pl.BlockSpec((tm, d), row), pl.BlockSpec((tm, d), row),
            _resident((1, d)),
            _resident((d, 3 * d)),
            _resident((1, 3 * d)),
            _resident((d, d)),
            _resident((1, d)),
        ],
        out_specs=pl.BlockSpec((tm, d), row),
        out_shape=jax.ShapeDtypeStruct((t, d), F32),
        compiler_params=_params(("arbitrary",)),
        name="merge_out",
    )(x2, yc, ya, ym, g1, w_gate, b_gate, w_out, g_post)


FFN_FC = 256


def _ffn_kernel(x_ref, g2_ref, wu_ref, cw_ref, cb_ref, wd_ref, gp_ref, o_ref,
                ext_ref, tail_ref, *, tm, tiles_per_seq):
    first = pl.program_id(0) % tiles_per_seq == 0
    halo = SUBLANES
    f = wd_ref.shape[0]
    fc = FFN_FC
    x = x_ref[...]
    h = _rms(x, g2_ref[...]).astype(BF16)

    @pl.when(pl.program_id(0) == 0)
    def _():
        tail_ref[...] = jnp.zeros(tail_ref.shape, F32)

    def conv_half(col0, slot, buf):
        cs = slice(col0, col0 + fc)
        up = _dot(h, wu_ref[:, cs])

        ext_ref[buf, 0:halo, :] = jnp.where(first, 0.0, tail_ref[slot])
        ext_ref[buf, halo:halo + tm, :] = up
        tail_ref[slot] = up[tm - halo:tm, :]
        u = cb_ref[:, cs] + cw_ref[2:3, cs] * up
        u = u + cw_ref[1:2, cs] * ext_ref[buf, halo - 1:halo - 1 + tm, :]
        u = u + cw_ref[0:1, cs] * ext_ref[buf, halo - 2:halo - 2 + tm, :]
        return u

    acc = None
    for j in range(f // fc):
        ug = conv_half(j * fc, 2 * j, 2 * (j % 2))
        uv = conv_half(f + j * fc, 2 * j + 1, 2 * (j % 2) + 1)
        act = (ug * jax.nn.sigmoid(ug)) * uv
        part = _dot(act.astype(BF16), wd_ref[j * fc:(j + 1) * fc, :])
        acc = part if acc is None else acc + part
    o_ref[...] = x + _rms(acc, gp_ref[...])


def _ffn(x1, g2, w_up, cw, cb, w_down, g_post, seq, tm):
    t, d = x1.shape
    f = w_down.shape[0]
    fc = FFN_FC
    kern = functools.partial(_ffn_kernel, tm=tm, tiles_per_seq=seq // tm)
    return pl.pallas_call(
        kern,
        grid=(t // tm,),
        in_specs=[
            pl.BlockSpec((tm, d), lambda i: (i, 0)),
            _resident((1, d)),
            _resident(w_up.shape),
            _resident(cw.shape),
            _resident(cb.shape),
            _resident(w_down.shape),
            _resident((1, d)),
        ],
        out_specs=pl.BlockSpec((tm, d), lambda i: (i, 0)),
        out_shape=jax.ShapeDtypeStruct((t, d), F32),
        scratch_shapes=[
            pltpu.VMEM((4, tm + SUBLANES, fc), F32),
            pltpu.VMEM((2 * (f // fc), SUBLANES, fc), F32),
        ],
        compiler_params=_params(("arbitrary",)),
        name="ffn",
    )(x1, g2, w_up, cw, cb, w_down, g_post)


def _pad_rows(a, rows):
    return jnp.pad(a, ((0, rows - a.shape[0]), (0, 0)))


def _pad_cols(a, cols):
    return jnp.pad(a, ((0, 0), (0, cols - a.shape[1])))


def _layer(x2, mem2, batch, seq, n_mem, p):
    d = D_MODEL
    row = lambda v: v.reshape(1, -1)
    w_in = p["w_in"]
    c2 = 2 * d
    o_q, o_qi = c2, c2 + 3 * d
    o_wi = o_qi + IDX_HEADS * IDX_HEAD_DIM
    o_ki = o_wi + IDX_HEADS
    o_qm = o_ki + IDX_HEAD_DIM

    w_glu = w_in[:, :c2].astype(BF16)
    w_qkvm = jnp.concatenate([w_in[:, o_q:o_qi], w_in[:, o_qm:]], axis=1).astype(BF16)
    w_idx = jnp.concatenate(
        [w_in[:, o_qi:o_wi], _pad_cols(w_in[:, o_ki:o_qm], LANES),
         _pad_cols(w_in[:, o_wi:o_ki], LANES)], axis=1)
    w_idx_hi = w_idx.astype(BF16)
    w_idx_lo = (w_idx - w_idx_hi.astype(F32)).astype(BF16)

    g1 = row(p["norm1_pre_g"])
    yg, qkvm, qcat, kcat, wi = _proj(x2, g1, w_glu, w_qkvm, w_idx_hi, w_idx_lo, tm=512)

    dw_w = _pad_rows(p["conv_dw_w"], 32)
    y_conv = _conv_branch(yg, dw_w, row(p["conv_dw_b"]), row(p["conv_ln_g"]),
                          row(p["conv_ln_b"]), p["conv_pw2"].astype(BF16), seq, ts=256)

    tk = DSA_TK
    tri = (lax.broadcasted_iota(jnp.int32, (tk, tk), 1)
           < lax.broadcasted_iota(jnp.int32, (tk, tk), 0)).astype(BF16)
    vt = qkvm[:, 2 * d:3 * d].reshape(batch * seq // tk, tk, d).transpose(0, 2, 1)
    wit = wi[:, :IDX_HEADS].T
    y_att = _dsa(qkvm, vt, qcat, kcat, wit, tri, batch, seq)

    mkv = _proj_bf16(mem2, row(p["mem_norm_g"]), p["w_mem_kv"].astype(BF16), tm=n_mem)
    y_mem = _mem_attn(qkvm, mkv, batch, seq, n_mem, tm=512)

    x1 = _merge(x2, y_conv, y_att, y_mem, g1, p["w_gate"].astype(BF16), row(p["b_gate"]),
                p["w_out"].astype(BF16), row(p["norm1_post_g"]), tm=256)

    x2o = _ffn(x1, row(p["norm2_pre_g"]), p["w_up"].astype(BF16),
               _pad_rows(p["ffn_dw_w"], SUBLANES), row(p["ffn_dw_b"]),
               p["w_down"].astype(BF16), row(p["norm2_post_g"]), seq, tm=512)
    return x2o


def kernel(x, mem, norm1_pre_g, w_in, conv_dw_w, conv_dw_b, conv_ln_g, conv_ln_b, conv_pw2,
           mem_norm_g, w_mem_kv, w_gate, b_gate, w_out, norm1_post_g, norm2_pre_g, w_up,
           ffn_dw_w, ffn_dw_b, w_down, norm2_post_g):
    batch, seq, d = x.shape
    n_mem = mem.shape[1]
    assert d == D_MODEL and seq % DSA_TK == 0 and n_mem % SUBLANES == 0
    names = ("norm1_pre_g", "w_in", "conv_dw_w", "conv_dw_b", "conv_ln_g", "conv_ln_b",
             "conv_pw2", "mem_norm_g", "w_mem_kv", "w_gate", "b_gate", "w_out", "norm1_post_g",
             "norm2_pre_g", "w_up", "ffn_dw_w", "ffn_dw_b", "w_down", "norm2_post_g")
    vals = (norm1_pre_g, w_in, conv_dw_w, conv_dw_b, conv_ln_g, conv_ln_b, conv_pw2,
            mem_norm_g, w_mem_kv, w_gate, b_gate, w_out, norm1_post_g, norm2_pre_g, w_up,
            ffn_dw_w, ffn_dw_b, w_down, norm2_post_g)
    x2 = x.reshape(batch * seq, d)
    mem2 = mem.reshape(batch * n_mem, d)
    for l in range(norm1_pre_g.shape[0]):
        x2 = _layer(x2, mem2, batch, seq, n_mem, {n: v[l] for n, v in zip(names, vals)})
    return x2.reshape(batch, seq, d)
```

```python
import functools

import jax
import jax.numpy as jnp
from jax import lax
from jax.experimental import pallas as pl
from jax.experimental.pallas import tpu as pltpu

EPS = 1e-6
D_MODEL = 1024
CONV_K = 31
ATT_HEADS = 8
ATT_HEAD_DIM = 128
IDX_HEADS = 8
IDX_HEAD_DIM = 64
TOPK_MAX = 256
MEM_HEADS = 4
MEM_HEAD_DIM = 256
FFN_DIM = 2816
FFN_CONV_K = 3

LANES = 128
SUBLANES = 8
VMEM_LIMIT_BYTES = 56 * 1024 * 1024

F32 = jnp.float32
BF16 = jnp.bfloat16
NEG_INF = float("-inf")
POS_INF = float("inf")


def _params(semantics):
    return pltpu.CompilerParams(dimension_semantics=semantics,
                                vmem_limit_bytes=VMEM_LIMIT_BYTES)


def _resident(shape):
    return pl.BlockSpec(shape, lambda *_: (0,) * len(shape), pipeline_mode=pl.Buffered(1))


def _dot(a, b):
    return jnp.dot(a, b, preferred_element_type=F32)


def _dot_nt(a, b):
    return lax.dot_general(a, b, (((1,), (1,)), ((), ())), preferred_element_type=F32)


def _rms(x, g):
    return x * lax.rsqrt(jnp.mean(x * x, axis=-1, keepdims=True) + EPS) * g


def _split_bf16(v):
    hi = v.astype(BF16).astype(F32)
    lo = (v - hi).astype(BF16).astype(F32)
    return hi, lo


PROJ_TN = 512
Q_PRESCALE = ATT_HEAD_DIM ** -0.5 * 1.4426950408889634
IDX_CAT = 2 * LANES


def _proj_kernel(x_ref, g_ref, wglu_ref, wqkm_ref, wvt_ref, whi_ref, wlo_ref,
                 yg_ref, qkm_ref, vt_ref, qcat_ref, kcat_ref, wi_ref):
    h = _rms(x_ref[...], g_ref[...])
    h_hi = h.astype(BF16)
    h_lo = (h - h_hi.astype(F32)).astype(BF16)

    c = yg_ref.shape[1]
    for j in range(c // PROJ_TN):
        a = _dot(h_hi, wglu_ref[:, j * PROJ_TN:(j + 1) * PROJ_TN])
        gate = _dot(h_hi, wglu_ref[:, c + j * PROJ_TN:c + (j + 1) * PROJ_TN])
        yg_ref[:, j * PROJ_TN:(j + 1) * PROJ_TN] = a * jax.nn.sigmoid(gate)

    for j in range(qkm_ref.shape[1] // PROJ_TN):
        cs = slice(j * PROJ_TN, (j + 1) * PROJ_TN)
        r = _dot(h_hi, wqkm_ref[:, cs])
        if (j + 1) * PROJ_TN <= ATT_HEADS * ATT_HEAD_DIM:
            r = r * Q_PRESCALE
        qkm_ref[:, cs] = r.astype(BF16)

    for j in range(vt_ref.shape[1] // PROJ_TN):
        cs = slice(j * PROJ_TN, (j + 1) * PROJ_TN)
        vt_ref[0, cs, :] = _dot_nt(wvt_ref[cs, :], h_hi).astype(BF16)

    whi = whi_ref[...]
    out = _dot(h_hi, whi) + _dot(h_lo, whi) + _dot(h_hi, wlo_ref[...])
    low_half = lax.broadcasted_iota(jnp.int32, (1, LANES), 1) < IDX_HEAD_DIM
    npair = IDX_HEADS // 2
    for p in range(npair):
        hi, lo = _split_bf16(out[:, p * LANES:(p + 1) * LANES])
        hi_sw = pltpu.roll(hi, IDX_HEAD_DIM, 1)
        lo_sw = pltpu.roll(lo, IDX_HEAD_DIM, 1)
        base = 2 * p * IDX_CAT
        qcat_ref[:, base:base + LANES] = jnp.where(low_half, hi, lo_sw).astype(BF16)
        qcat_ref[:, base + LANES:base + 2 * LANES] = jnp.where(low_half, hi, 0.0).astype(BF16)
        qcat_ref[:, base + 2 * LANES:base + 3 * LANES] = jnp.where(low_half, hi_sw, lo).astype(BF16)
        qcat_ref[:, base + 3 * LANES:base + 4 * LANES] = jnp.where(low_half, hi_sw, 0.0).astype(BF16)
    kb = npair * LANES
    k_hi, k_lo = _split_bf16(out[:, kb:kb + LANES])
    kcat_ref[:, 0:LANES] = (k_hi + pltpu.roll(k_hi, IDX_HEAD_DIM, 1)).astype(BF16)
    kcat_ref[:, LANES:2 * LANES] = k_lo.astype(BF16)
    wi_ref[...] = out[:, kb + LANES:kb + 2 * LANES]


def _proj(x2, g, w_glu, w_qkm, w_vt, w_hi, w_lo):
    tm = DSA_TK
    t, d = x2.shape
    c = w_glu.shape[1] // 2
    nq = w_qkm.shape[1]
    nv = w_vt.shape[0]
    row = lambda i: (i, 0)
    return pl.pallas_call(
        _proj_kernel,
        grid=(t // tm,),
        in_specs=[
            pl.BlockSpec((tm, d), row),
            _resident((1, d)),
            _resident(w_glu.shape),
            _resident(w_qkm.shape),
            _resident(w_vt.shape),
            _resident(w_hi.shape),
            _resident(w_lo.shape),
        ],
        out_specs=[
            pl.BlockSpec((tm, c), row),
            pl.BlockSpec((tm, nq), row),
            pl.BlockSpec((1, nv, tm), lambda i: (i, 0, 0)),
            pl.BlockSpec((tm, IDX_HEADS * IDX_CAT), row),
            pl.BlockSpec((tm, IDX_CAT), row),
            pl.BlockSpec((tm, LANES), row),
        ],
        out_shape=[
            jax.ShapeDtypeStruct((t, c), F32),
            jax.ShapeDtypeStruct((t, nq), BF16),
            jax.ShapeDtypeStruct((t // tm, nv, tm), BF16),
            jax.ShapeDtypeStruct((t, IDX_HEADS * IDX_CAT), BF16),
            jax.ShapeDtypeStruct((t, IDX_CAT), BF16),
            jax.ShapeDtypeStruct((t, LANES), F32),
        ],
        compiler_params=_params(("arbitrary",)),
        name="proj",
    )(x2, g, w_glu, w_qkm, w_vt, w_hi, w_lo)


def _proj_bf16_kernel(x_ref, g_ref, w_ref, o_ref):
    h = _rms(x_ref[...], g_ref[...]).astype(BF16)
    o_ref[...] = _dot(h, w_ref[...]).astype(BF16)


def _proj_bf16(x2, g, w, tm):
    t, d = x2.shape
    n = w.shape[1]
    return pl.pallas_call(
        _proj_bf16_kernel,
        grid=(t // tm,),
        in_specs=[
            pl.BlockSpec((tm, d), lambda i: (i, 0)),
            _resident((1, d)),
            _resident(w.shape),
        ],
        out_specs=pl.BlockSpec((tm, n), lambda i: (i, 0)),
        out_shape=jax.ShapeDtypeStruct((t, n), BF16),
        compiler_params=_params(("arbitrary",)),
        name="proj_mem",
    )(x2, g, w)


CONV_HALO = 32
CONV_ROWS = 64


def _conv_branch_kernel(y_ref, dww_ref, dwb_ref, lng_ref, lnb_ref, pw2_ref, o_ref,
                        ext_ref, sh_ref, cv_ref, *, ts, tiles_per_seq):
    c = y_ref.shape[1]
    first = pl.program_id(0) % tiles_per_seq == 0

    @pl.when(first)
    def _():
        ext_ref[0:CONV_HALO, :] = jnp.zeros((CONV_HALO, c), F32)

    @pl.when(jnp.logical_not(first))
    def _():
        ext_ref[0:CONV_HALO, :] = ext_ref[ts:ts + CONV_HALO, :]

    ext_ref[CONV_HALO:CONV_HALO + ts, :] = y_ref[...]

    off0 = CONV_HALO - (CONV_K - 1)
    span = ts + CONV_HALO - SUBLANES
    for ph in range(1, SUBLANES):
        sh_ref[ph - 1, 0:span, :] = ext_ref[ph:ph + span, :]
    for cg in range(c // LANES):
        cs = slice(cg * LANES, (cg + 1) * LANES)
        for rb in range(ts // CONV_ROWS):
            r0 = rb * CONV_ROWS
            acc = jnp.broadcast_to(dwb_ref[:, cs], (CONV_ROWS, LANES))
            for j in range(CONV_K):
                ph = (off0 + j) % SUBLANES
                a0 = r0 + off0 + j - ph
                if ph == 0:
                    tap = ext_ref[a0:a0 + CONV_ROWS, cs]
                else:
                    tap = sh_ref[ph - 1, a0:a0 + CONV_ROWS, cs]
                acc = acc + dww_ref[j:j + 1, cs] * tap
            cv_ref[r0:r0 + CONV_ROWS, cs] = acc

    y = cv_ref[...]
    mu = jnp.mean(y, axis=-1, keepdims=True)
    yc = y - mu
    var = jnp.mean(yc * yc, axis=-1, keepdims=True)
    z = yc * lax.rsqrt(var + EPS) * lng_ref[...] + lnb_ref[...]
    z = z * jax.nn.sigmoid(z)
    o_ref[...] = _dot(z.astype(BF16), pw2_ref[...])


def _conv_branch(yg, dw_w, dw_b, ln_g, ln_b, pw2, seq, ts):
    t, c = yg.shape
    kern = functools.partial(_conv_branch_kernel, ts=ts, tiles_per_seq=seq // ts)
    return pl.pallas_call(
        kern,
        grid=(t // ts,),
        in_specs=[
            pl.BlockSpec((ts, c), lambda i: (i, 0)),
            _resident(dw_w.shape),
            _resident((1, c)),
            _resident((1, c)),
            _resident((1, c)),
            _resident((c, c)),
        ],
        out_specs=pl.BlockSpec((ts, c), lambda i: (i, 0)),
        out_shape=jax.ShapeDtypeStruct((t, c), F32),
        scratch_shapes=[pltpu.VMEM((ts + CONV_HALO, c), F32),
                        pltpu.VMEM((SUBLANES - 1, ts + CONV_HALO, c), F32),
                        pltpu.VMEM((ts, c), F32)],
        compiler_params=_params(("arbitrary",)),
        name="conv_branch",
    )(yg, dw_w, dw_b, ln_g, ln_b, pw2)


DSA_TQ = 256
DSA_TK = 512
DSA_NACC = 4
ATT_NACC = 2
DSA_HG = 2
BISECT_LINEAR = 24
BISECT_UNROLL = 2
BISECT_MAX = 96


def _ordered_key(f):
    b = lax.bitcast_convert_type(f, jnp.int32)
    return b ^ ((b >> 31) & jnp.int32(0x7FFFFFFF))


def _from_ordered_key(k):
    return lax.bitcast_convert_type(k ^ ((k >> 31) & jnp.int32(0x7FFFFFFF)), F32)


def _dsa_kernel(q_ref, k_ref, vt_ref, qcat_ref, kcat_ref, wit_ref, tri_ref, o_ref,
                sc_ref, lg_ref, acc_ref, *, topk):
    tq, tk = DSA_TQ, DSA_TK
    qt = pl.program_id(1)
    nck = ((qt + 1) * tq + tk - 1) // tk
    idx_scale = (IDX_HEADS ** -0.5) * (IDX_HEAD_DIM ** -0.5)
    kf = float(topk)

    q_pos = qt * tq + lax.broadcasted_iota(jnp.int32, (1, tq), 1)
    key_off = lax.broadcasted_iota(jnp.int32, (tk, 1), 0)

    def score_chunk(c, carry):
        mn, mx = carry
        r0 = pl.multiple_of(c * tk, tk)
        kc = kcat_ref[pl.ds(r0, tk), :]
        acc = jnp.zeros((tk, tq), F32)
        for h in range(IDX_HEADS):
            d = _dot_nt(kc, qcat_ref[:, h * IDX_CAT:(h + 1) * IDX_CAT])
            acc = acc + jnp.maximum(d, 0.0) * wit_ref[h:h + 1, :]
        s = acc * idx_scale
        causal = (c * tk + key_off) <= q_pos
        lo_s = jnp.where(causal, s, POS_INF)
        hi_s = jnp.where(causal, s, NEG_INF)
        sc_ref[c] = hi_s
        mn = jnp.minimum(mn, jnp.min(lo_s, axis=0, keepdims=True))
        mx = jnp.maximum(mx, jnp.max(hi_s, axis=0, keepdims=True))
        return mn, mx

    row_min, row_max = lax.fori_loop(
        0, nck, score_chunk,
        (jnp.full((1, tq), POS_INF, F32), jnp.full((1, tq), NEG_INF, F32)))

    def count_rows(cand, strict):
        cb = jnp.broadcast_to(cand, (SUBLANES, tq))

        def body(c, parts):
            parts = list(parts)
            for i in range(tk // SUBLANES):
                s = sc_ref[c, i * SUBLANES:(i + 1) * SUBLANES, :]
                hit = (s > cb) if strict else (s >= cb)
                a = i % DSA_NACC
                parts[a] = jnp.where(hit, parts[a] + 1.0, parts[a])
            return tuple(parts)

        parts = lax.fori_loop(0, nck, body,
                              tuple(jnp.zeros((SUBLANES, tq), F32) for _ in range(DSA_NACC)))
        return jnp.sum(functools.reduce(jnp.add, parts), axis=0, keepdims=True)

    n_valid = (q_pos + 1).astype(F32)
    few = n_valid <= kf
    zero = jnp.zeros((1, tq), F32)
    cnt_ge0 = count_rows(zero, strict=False)
    cnt_gt0 = count_rows(zero, strict=True)
    pos = cnt_ge0 >= kf
    lo0 = jnp.where(pos, 0.0, row_min)
    cnt_lo0 = jnp.where(pos, cnt_ge0, n_valid)
    hi0 = jnp.where(pos, _from_ordered_key(_ordered_key(row_max) + 1), 0.0)

    def is_done(lo, hi, cnt_lo, done):
        adjacent = _ordered_key(hi) <= _ordered_key(lo) + 1
        return jnp.where((cnt_lo == kf) | adjacent, 1.0, done)

    def not_done(state):
        it, active = state[0], state[1]
        return jnp.logical_and(active > 0, it < BISECT_MAX)

    def step(it, lo, hi, cnt_lo, done):
        klo, khi = _ordered_key(lo), _ordered_key(hi)
        mid_bits = _from_ordered_key((klo >> 1) + (khi >> 1) + (klo & khi & 1))
        mid = jnp.where(it < BISECT_LINEAR, 0.5 * lo + 0.5 * hi, mid_bits)
        cnt = count_rows(mid, strict=False)
        ge = cnt >= kf
        upd = done < 0.5
        lo = jnp.where(upd & ge, mid, lo)
        cnt_lo = jnp.where(upd & ge, cnt, cnt_lo)
        hi = jnp.where(upd & jnp.logical_not(ge), mid, hi)
        return lo, hi, cnt_lo, is_done(lo, hi, cnt_lo, done)

    def bisect(state):
        it, _, lo, hi, cnt_lo, done = state
        for _ in range(BISECT_UNROLL):
            lo, hi, cnt_lo, done = step(it, lo, hi, cnt_lo, done)
        active = jnp.max(1.0 - done).astype(jnp.int32)
        return it + BISECT_UNROLL, active, lo, hi, cnt_lo, done

    done0 = jnp.where(few | ((cnt_gt0 < kf) & pos), 1.0, 0.0)
    done0 = is_done(lo0, hi0, cnt_lo0, done0)
    active0 = jnp.max(1.0 - done0).astype(jnp.int32)
    state = lax.while_loop(not_done, bisect,
                           (jnp.int32(0), active0, lo0, hi0, cnt_lo0, done0))
    thr = jnp.where(few, NEG_INF, state[2])

    cnt_gt = count_rows(thr, strict=True)
    need = jnp.where(few, 0.0, kf - cnt_gt)

    def bias_chunk(c, seen):
        s = sc_ref[c]
        eq = s == thr
        eqf = jnp.where(eq, 1.0, 0.0)
        before = _dot(tri_ref[...], eqf.astype(BF16)) + seen
        keep = (s > thr) | (eq & (before < need))
        sc_ref[c] = jnp.where(keep, 0.0, NEG_INF)
        return seen + jnp.sum(eqf, axis=0, keepdims=True)

    lax.fori_loop(0, nck, bias_chunk, jnp.zeros((1, tq), F32))

    def fold_rows(x, op, parts):
        parts = list(parts)
        for i in range(tk // SUBLANES):
            a = i % ATT_NACC
            parts[a] = op(parts[a], x[i * SUBLANES:(i + 1) * SUBLANES, :])
        return tuple(parts)

    def init_parts(value):
        return tuple(tuple(jnp.full((SUBLANES, tq), value, F32) for _ in range(ATT_NACC))
                     for _ in range(DSA_HG))

    def head_cols(grp, i):
        h = grp * DSA_HG + i
        return slice(h * ATT_HEAD_DIM, (h + 1) * ATT_HEAD_DIM)

    def make_body(grp_l, grp_p, m_rows):
        def body(c, carry):
            mparts, lparts = carry
            r0 = pl.multiple_of(c * tk, tk)
            if grp_l is not None:
                new_m = []
                for i in range(DSA_HG):
                    hs = head_cols(grp_l, i)
                    l = _dot_nt(k_ref[pl.ds(r0, tk), hs], q_ref[:, hs]) + sc_ref[c]
                    lg_ref[grp_l % 2, i, c] = l
                    new_m.append(fold_rows(l, jnp.maximum, mparts[i]))
                mparts = tuple(new_m)
            if grp_p is not None:
                new_l = []
                for i in range(DSA_HG):
                    hs = head_cols(grp_p, i)
                    p = jnp.exp2(lg_ref[grp_p % 2, i, c] - m_rows[i])
                    new_l.append(fold_rows(p, jnp.add, lparts[i]))
                    acc_ref[i] = acc_ref[i] + _dot(vt_ref[c, hs, :], p.astype(BF16))
                lparts = tuple(new_l)
            return mparts, lparts
        return body

    ngrp = ATT_HEADS // DSA_HG
    m_rows = None
    for phase in range(ngrp + 1):
        grp_l = phase if phase < ngrp else None
        grp_p = phase - 1 if phase > 0 else None
        if grp_p is not None:
            acc_ref[...] = jnp.zeros(acc_ref.shape, F32)
        mparts, lparts = lax.fori_loop(0, nck, make_body(grp_l, grp_p, m_rows),
                                       (init_parts(NEG_INF), init_parts(0.0)))
        if grp_p is not None:
            for i in range(DSA_HG):
                lsum = jnp.sum(functools.reduce(jnp.add, lparts[i]), axis=0, keepdims=True)
                o_ref[:, head_cols(grp_p, i)] = (acc_ref[i] / lsum).T
        if grp_l is not None:
            m_rows = [jnp.max(functools.reduce(jnp.maximum, mp), axis=0, keepdims=True)
                      for mp in mparts]


def _dsa(qkm, vt, qcat, kcat, wit, tri, batch, seq):
    tq, tk = DSA_TQ, DSA_TK
    d = D_MODEL
    nq = seq // tq
    nc = seq // tk
    topk = min(TOPK_MAX, seq // 4)
    kern = functools.partial(_dsa_kernel, topk=topk)
    qrow = lambda b, i: (b * nq + i, 0)
    return pl.pallas_call(
        kern,
        grid=(batch, nq),
        in_specs=[
            pl.BlockSpec((tq, d), qrow),
            pl.BlockSpec((seq, d), lambda b, i: (b, 1), pipeline_mode=pl.Buffered(1)),
            pl.BlockSpec((nc, d, tk), lambda b, i: (b, 0, 0), pipeline_mode=pl.Buffered(1)),
            pl.BlockSpec((tq, qcat.shape[1]), qrow),
            pl.BlockSpec((seq, kcat.shape[1]), lambda b, i: (b, 0)),
            pl.BlockSpec((IDX_HEADS, tq), lambda b, i: (0, b * nq + i)),
            _resident((tk, tk)),
        ],
        out_specs=pl.BlockSpec((tq, d), qrow),
        out_shape=jax.ShapeDtypeStruct((batch * seq, d), F32),
        scratch_shapes=[
            pltpu.VMEM((nc, tk, tq), F32),
            pltpu.VMEM((2, DSA_HG, nc, tk, tq), F32),
            pltpu.VMEM((DSA_HG, ATT_HEAD_DIM, tq), F32),
        ],
        compiler_params=_params(("arbitrary", "arbitrary")),
        name="dsa",
    )(qkm, qkm, vt, qcat, kcat, wit, tri)


def _mem_attn_kernel(qm_ref, mk_ref, mv_ref, o_ref):
    scale = MEM_HEAD_DIM ** -0.5
    for h in range(MEM_HEADS):
        hs = slice(h * MEM_HEAD_DIM, (h + 1) * MEM_HEAD_DIM)
        l = _dot_nt(qm_ref[:, hs], mk_ref[:, hs]) * scale
        p = jnp.exp(l - jnp.max(l, axis=1, keepdims=True))
        acc = _dot(p.astype(BF16), mv_ref[:, hs])
        o_ref[:, hs] = acc / jnp.sum(p, axis=1, keepdims=True)


def _mem_attn(qkm, mkv, batch, seq, n_mem, tm):
    d = D_MODEL
    nt = seq // tm
    return pl.pallas_call(
        _mem_attn_kernel,
        grid=(batch, nt),
        in_specs=[
            pl.BlockSpec((tm, d), lambda b, i: (b * nt + i, 2)),
            pl.BlockSpec((n_mem, d), lambda b, i: (b, 0)),
            pl.BlockSpec((n_mem, d), lambda b, i: (b, 1)),
        ],
        out_specs=pl.BlockSpec((tm, d), lambda b, i: (b * nt + i, 0)),
        out_shape=jax.ShapeDtypeStruct((batch * seq, d), F32),
        compiler_params=_params(("arbitrary", "arbitrary")),
        name="mem_attn",
    )(qkm, mkv, mkv)


def _merge_kernel(x_ref, yc_ref, ya_ref, ym_ref, g1_ref, wg_ref, bg_ref, wo_ref, gp_ref, o_ref):
    d = x_ref.shape[1]
    x = x_ref[...]
    h = _rms(x, g1_ref[...]).astype(BF16)
    merged = jnp.zeros(x.shape, F32)
    for br, y_ref in enumerate((yc_ref, ya_ref, ym_ref)):
        cs = slice(br * d, (br + 1) * d)
        gate = jax.nn.sigmoid(_dot(h, wg_ref[:, cs]) + bg_ref[:, cs])
        merged = merged + gate * y_ref[...]
    out = _dot(merged.astype(BF16), wo_ref[...])
    o_ref[...] = x + _rms(out, gp_ref[...])


def _merge(x2, yc, ya, ym, g1, w_gate, b_gate, w_out, g_post, tm):
    t, d = x2.shape
    row = lambda i: (i, 0)
    return pl.pallas_call(
        _merge_kernel,
        grid=(t // tm,),
        in_specs=[
            pl.BlockSpec((tm, d), row), pl.BlockSpec((tm, d), row),
            pl.BlockSpec((tm, d), row), pl.BlockSpec((tm, d), row),
            _resident((1, d)),
            _resident((d, 3 * d)),
            _resident((1, 3 * d)),
            _resident((d, d)),
            _resident((1, d)),
        ],
        out_specs=pl.BlockSpec((tm, d), row),
        out_shape=jax.ShapeDtypeStruct((t, d), F32),
        compiler_params=_params(("arbitrary",)),
        name="merge_out",
    )(x2, yc, ya, ym, g1, w_gate, b_gate, w_out, g_post)


FFN_FC = 256


def _ffn_kernel(x_ref, g2_ref, wu_ref, cw_ref, cb_ref, wd_ref, gp_ref, o_ref,
                ext_ref, tail_ref, *, tm, tiles_per_seq):
    first = pl.program_id(0) % tiles_per_seq == 0
    halo = SUBLANES
    f = wd_ref.shape[0]
    fc = FFN_FC
    x = x_ref[...]
    h = _rms(x, g2_ref[...]).astype(BF16)

    @pl.when(pl.program_id(0) == 0)
    def _():
        tail_ref[...] = jnp.zeros(tail_ref.shape, F32)

    def conv_half(col0, slot, buf):
        cs = slice(col0, col0 + fc)
        up = _dot(h, wu_ref[:, cs])

        ext_ref[buf, 0:halo, :] = jnp.where(first, 0.0, tail_ref[slot])
        ext_ref[buf, halo:halo + tm, :] = up
        tail_ref[slot] = up[tm - halo:tm, :]
        u = cb_ref[:, cs] + cw_ref[2:3, cs] * up
        u = u + cw_ref[1:2, cs] * ext_ref[buf, halo - 1:halo - 1 + tm, :]
        u = u + cw_ref[0:1, cs] * ext_ref[buf, halo - 2:halo - 2 + tm, :]
        return u

    acc = None
    for j in range(f // fc):
        ug = conv_half(j * fc, 2 * j, 2 * (j % 2))
        uv = conv_half(f + j * fc, 2 * j + 1, 2 * (j % 2) + 1)
        act = (ug * jax.nn.sigmoid(ug)) * uv
        part = _dot(act.astype(BF16), wd_ref[j * fc:(j + 1) * fc, :])
        acc = part if acc is None else acc + part
    o_ref[...] = x + _rms(acc, gp_ref[...])


def _ffn(x1, g2, w_up, cw, cb, w_down, g_post, seq, tm):
    t, d = x1.shape
    f = w_down.shape[0]
    fc = FFN_FC
    kern = functools.partial(_ffn_kernel, tm=tm, tiles_per_seq=seq // tm)
    return pl.pallas_call(
        kern,
        grid=(t // tm,),
        in_specs=[
            pl.BlockSpec((tm, d), lambda i: (i, 0)),
            _resident((1, d)),
            _resident(w_up.shape),
            _resident(cw.shape),
            _resident(cb.shape),
            _resident(w_down.shape),
            _resident((1, d)),
        ],
        out_specs=pl.BlockSpec((tm, d), lambda i: (i, 0)),
        out_shape=jax.ShapeDtypeStruct((t, d), F32),
        scratch_shapes=[
            pltpu.VMEM((4, tm + SUBLANES, fc), F32),
            pltpu.VMEM((2 * (f // fc), SUBLANES, fc), F32),
        ],
        compiler_params=_params(("arbitrary",)),
        name="ffn",
    )(x1, g2, w_up, cw, cb, w_down, g_post)


def _pad_rows(a, rows):
    return jnp.pad(a, ((0, rows - a.shape[0]), (0, 0)))


def _pad_cols(a, cols):
    return jnp.pad(a, ((0, 0), (0, cols - a.shape[1])))


def _layer(x2, mem2, batch, seq, n_mem, p):
    d = D_MODEL
    row = lambda v: v.reshape(1, -1)
    w_in = p["w_in"]
    c2 = 2 * d
    o_q, o_qi = c2, c2 + 3 * d
    o_wi = o_qi + IDX_HEADS * IDX_HEAD_DIM
    o_ki = o_wi + IDX_HEADS
    o_qm = o_ki + IDX_HEAD_DIM

    w_glu = w_in[:, :c2].astype(BF16)
    w_qkm = jnp.concatenate([w_in[:, o_q:o_q + 2 * d], w_in[:, o_qm:]], axis=1).astype(BF16)
    w_vt = w_in[:, o_q + 2 * d:o_qi].T.astype(BF16)
    w_idx = jnp.concatenate(
        [w_in[:, o_qi:o_wi], _pad_cols(w_in[:, o_ki:o_qm], LANES),
         _pad_cols(w_in[:, o_wi:o_ki], LANES)], axis=1)
    w_idx_hi = w_idx.astype(BF16)
    w_idx_lo = (w_idx - w_idx_hi.astype(F32)).astype(BF16)

    g1 = row(p["norm1_pre_g"])
    yg, qkm, vt, qcat, kcat, wi = _proj(x2, g1, w_glu, w_qkm, w_vt, w_idx_hi, w_idx_lo)

    dw_w = _pad_rows(p["conv_dw_w"], 32)
    y_conv = _conv_branch(yg, dw_w, row(p["conv_dw_b"]), row(p["conv_ln_g"]),
                          row(p["conv_ln_b"]), p["conv_pw2"].astype(BF16), seq, ts=256)

    tk = DSA_TK
    tri = (lax.broadcasted_iota(jnp.int32, (tk, tk), 1)
           < lax.broadcasted_iota(jnp.int32, (tk, tk), 0)).astype(BF16)
    wit = wi[:, :IDX_HEADS].T
    y_att = _dsa(qkm, vt, qcat, kcat, wit, tri, batch, seq)

    mkv = _proj_bf16(mem2, row(p["mem_norm_g"]), p["w_mem_kv"].astype(BF16), tm=n_mem)
    y_mem = _mem_attn(qkm, mkv, batch, seq, n_mem, tm=512)

    x1 = _merge(x2, y_conv, y_att, y_mem, g1, p["w_gate"].astype(BF16), row(p["b_gate"]),
                p["w_out"].astype(BF16), row(p["norm1_post_g"]), tm=256)

    x2o = _ffn(x1, row(p["norm2_pre_g"]), p["w_up"].astype(BF16),
               _pad_rows(p["ffn_dw_w"], SUBLANES), row(p["ffn_dw_b"]),
               p["w_down"].astype(BF16), row(p["norm2_post_g"]), seq, tm=512)
    return x2o


def kernel(x, mem, norm1_pre_g, w_in, conv_dw_w, conv_dw_b, conv_ln_g, conv_ln_b, conv_pw2,
           mem_norm_g, w_mem_kv, w_gate, b_gate, w_out, norm1_post_g, norm2_pre_g, w_up,
           ffn_dw_w, ffn_dw_b, w_down, norm2_post_g):
    batch, seq, d = x.shape
    n_mem = mem.shape[1]
    assert d == D_MODEL and seq % DSA_TK == 0 and n_mem % SUBLANES == 0
    names = ("norm1_pre_g", "w_in", "conv_dw_w", "conv_dw_b", "conv_ln_g", "conv_ln_b",
             "conv_pw2", "mem_norm_g", "w_mem_kv", "w_gate", "b_gate", "w_out", "norm1_post_g",
             "norm2_pre_g", "w_up", "ffn_dw_w", "ffn_dw_b", "w_down", "norm2_post_g")
    vals = (norm1_pre_g, w_in, conv_dw_w, conv_dw_b, conv_ln_g, conv_ln_b, conv_pw2,
            mem_norm_g, w_mem_kv, w_gate, b_gate, w_out, norm1_post_g, norm2_pre_g, w_up,
            ffn_dw_w, ffn_dw_b, w_down, norm2_post_g)
    x2 = x.reshape(batch * seq, d)
    mem2 = mem.reshape(batch * n_mem, d)
    for l in range(norm1_pre_g.shape[0]):
        x2 = _layer(x2, mem2, batch, seq, n_mem, {n: v[l] for n, v in zip(names, vals)})
    return x2.reshape(batch, seq, d)
```

```python
import functools

import jax
import jax.numpy as jnp
from jax import lax
from jax.experimental import pallas as pl
from jax.experimental.pallas import tpu as pltpu

EPS = 1e-6
D_MODEL = 1024
CONV_K = 31
ATT_HEADS = 8
ATT_HEAD_DIM = 128
IDX_HEADS = 8
IDX_HEAD_DIM = 64
TOPK_MAX = 256
MEM_HEADS = 4
MEM_HEAD_DIM = 256
FFN_DIM = 2816
FFN_CONV_K = 3

LANES = 128
SUBLANES = 8
VMEM_LIMIT_BYTES = 56 * 1024 * 1024

F32 = jnp.float32
BF16 = jnp.bfloat16
NEG_INF = float("-inf")
POS_INF = float("inf")


def _params(semantics):
    return pltpu.CompilerParams(dimension_semantics=semantics,
                                vmem_limit_bytes=VMEM_LIMIT_BYTES)


def _resident(shape):
    return pl.BlockSpec(shape, lambda *_: (0,) * len(shape), pipeline_mode=pl.Buffered(1))


def _dot(a, b):
    return jnp.dot(a, b, preferred_element_type=F32)


def _dot_nt(a, b):
    return lax.dot_general(a, b, (((1,), (1,)), ((), ())), preferred_element_type=F32)


def _rms(x, g):
    return x * lax.rsqrt(jnp.mean(x * x, axis=-1, keepdims=True) + EPS) * g


def _split_bf16(v):
    hi = v.astype(BF16).astype(F32)
    lo = (v - hi).astype(BF16).astype(F32)
    return hi, lo


PROJ_TN = 512
Q_PRESCALE = ATT_HEAD_DIM ** -0.5 * 1.4426950408889634
IDX_CAT = 2 * LANES


def _proj_kernel(x_ref, g_ref, wglu_ref, wqkm_ref, wvt_ref, whi_ref, wlo_ref,
                 yg_ref, qkm_ref, vt_ref, qcat_ref, kcat_ref, wi_ref):
    h = _rms(x_ref[...], g_ref[...])
    h_hi = h.astype(BF16)
    h_lo = (h - h_hi.astype(F32)).astype(BF16)

    c = yg_ref.shape[1]
    for j in range(c // PROJ_TN):
        a = _dot(h_hi, wglu_ref[:, j * PROJ_TN:(j + 1) * PROJ_TN])
        gate = _dot(h_hi, wglu_ref[:, c + j * PROJ_TN:c + (j + 1) * PROJ_TN])
        yg_ref[:, j * PROJ_TN:(j + 1) * PROJ_TN] = a * jax.nn.sigmoid(gate)

    for j in range(qkm_ref.shape[1] // PROJ_TN):
        cs = slice(j * PROJ_TN, (j + 1) * PROJ_TN)
        r = _dot(h_hi, wqkm_ref[:, cs])
        if (j + 1) * PROJ_TN <= ATT_HEADS * ATT_HEAD_DIM:
            r = r * Q_PRESCALE
        qkm_ref[:, cs] = r.astype(BF16)

    for j in range(vt_ref.shape[1] // PROJ_TN):
        cs = slice(j * PROJ_TN, (j + 1) * PROJ_TN)
        vt_ref[0, cs, :] = _dot_nt(wvt_ref[cs, :], h_hi).astype(BF16)

    whi = whi_ref[...]
    out = _dot(h_hi, whi) + _dot(h_lo, whi) + _dot(h_hi, wlo_ref[...])
    low_half = lax.broadcasted_iota(jnp.int32, (1, LANES), 1) < IDX_HEAD_DIM
    npair = IDX_HEADS // 2
    for p in range(npair):
        hi, lo = _split_bf16(out[:, p * LANES:(p + 1) * LANES])
        hi_sw = pltpu.roll(hi, IDX_HEAD_DIM, 1)
        lo_sw = pltpu.roll(lo, IDX_HEAD_DIM, 1)
        base = 2 * p * IDX_CAT
        qcat_ref[:, base:base + LANES] = jnp.where(low_half, hi, lo_sw).astype(BF16)
        qcat_ref[:, base + LANES:base + 2 * LANES] = jnp.where(low_half, hi, 0.0).astype(BF16)
        qcat_ref[:, base + 2 * LANES:base + 3 * LANES] = jnp.where(low_half, hi_sw, lo).astype(BF16)
        qcat_ref[:, base + 3 * LANES:base + 4 * LANES] = jnp.where(low_half, hi_sw, 0.0).astype(BF16)
    kb = npair * LANES
    k_hi, k_lo = _split_bf16(out[:, kb:kb + LANES])
    kcat_ref[:, 0:LANES] = (k_hi + pltpu.roll(k_hi, IDX_HEAD_DIM, 1)).astype(BF16)
    kcat_ref[:, LANES:2 * LANES] = k_lo.astype(BF16)
    wi_ref[...] = out[:, kb + LANES:kb + 2 * LANES]


def _proj(x2, g, w_glu, w_qkm, w_vt, w_hi, w_lo):
    tm = DSA_TK
    t, d = x2.shape
    c = w_glu.shape[1] // 2
    nq = w_qkm.shape[1]
    nv = w_vt.shape[0]
    row = lambda i: (i, 0)
    return pl.pallas_call(
        _proj_kernel,
        grid=(t // tm,),
        in_specs=[
            pl.BlockSpec((tm, d), row),
            _resident((1, d)),
            _resident(w_glu.shape),
            _resident(w_qkm.shape),
            _resident(w_vt.shape),
            _resident(w_hi.shape),
            _resident(w_lo.shape),
        ],
        out_specs=[
            pl.BlockSpec((tm, c), row),
            pl.BlockSpec((tm, nq), row),
            pl.BlockSpec((1, nv, tm), lambda i: (i, 0, 0)),
            pl.BlockSpec((tm, IDX_HEADS * IDX_CAT), row),
            pl.BlockSpec((tm, IDX_CAT), row),
            pl.BlockSpec((tm, LANES), row),
        ],
        out_shape=[
            jax.ShapeDtypeStruct((t, c), F32),
            jax.ShapeDtypeStruct((t, nq), BF16),
            jax.ShapeDtypeStruct((t // tm, nv, tm), BF16),
            jax.ShapeDtypeStruct((t, IDX_HEADS * IDX_CAT), BF16),
            jax.ShapeDtypeStruct((t, IDX_CAT), BF16),
            jax.ShapeDtypeStruct((t, LANES), F32),
        ],
        compiler_params=_params(("arbitrary",)),
        name="proj",
    )(x2, g, w_glu, w_qkm, w_vt, w_hi, w_lo)


def _proj_bf16_kernel(x_ref, g_ref, w_ref, o_ref):
    h = _rms(x_ref[...], g_ref[...]).astype(BF16)
    o_ref[...] = _dot(h, w_ref[...]).astype(BF16)


def _proj_bf16(x2, g, w, tm):
    t, d = x2.shape
    n = w.shape[1]
    return pl.pallas_call(
        _proj_bf16_kernel,
        grid=(t // tm,),
        in_specs=[
            pl.BlockSpec((tm, d), lambda i: (i, 0)),
            _resident((1, d)),
            _resident(w.shape),
        ],
        out_specs=pl.BlockSpec((tm, n), lambda i: (i, 0)),
        out_shape=jax.ShapeDtypeStruct((t, n), BF16),
        compiler_params=_params(("arbitrary",)),
        name="proj_mem",
    )(x2, g, w)


CONV_HALO = 32
CONV_ROWS = 64


def _conv_branch_kernel(y_ref, dww_ref, dwb_ref, lng_ref, lnb_ref, pw2_ref, o_ref,
                        ext_ref, sh_ref, cv_ref, *, ts, tiles_per_seq):
    c = y_ref.shape[1]
    first = pl.program_id(0) % tiles_per_seq == 0

    @pl.when(first)
    def _():
        ext_ref[0:CONV_HALO, :] = jnp.zeros((CONV_HALO, c), F32)

    @pl.when(jnp.logical_not(first))
    def _():
        ext_ref[0:CONV_HALO, :] = ext_ref[ts:ts + CONV_HALO, :]

    ext_ref[CONV_HALO:CONV_HALO + ts, :] = y_ref[...]

    off0 = CONV_HALO - (CONV_K - 1)
    span = ts + CONV_HALO - SUBLANES
    for ph in range(1, SUBLANES):
        sh_ref[ph - 1, 0:span, :] = ext_ref[ph:ph + span, :]
    for cg in range(c // LANES):
        cs = slice(cg * LANES, (cg + 1) * LANES)
        for rb in range(ts // CONV_ROWS):
            r0 = rb * CONV_ROWS
            acc = jnp.broadcast_to(dwb_ref[:, cs], (CONV_ROWS, LANES))
            for j in range(CONV_K):
                ph = (off0 + j) % SUBLANES
                a0 = r0 + off0 + j - ph
                if ph == 0:
                    tap = ext_ref[a0:a0 + CONV_ROWS, cs]
                else:
                    tap = sh_ref[ph - 1, a0:a0 + CONV_ROWS, cs]
                acc = acc + dww_ref[j:j + 1, cs] * tap
            cv_ref[r0:r0 + CONV_ROWS, cs] = acc

    y = cv_ref[...]
    mu = jnp.mean(y, axis=-1, keepdims=True)
    yc = y - mu
    var = jnp.mean(yc * yc, axis=-1, keepdims=True)
    z = yc * lax.rsqrt(var + EPS) * lng_ref[...] + lnb_ref[...]
    z = z * jax.nn.sigmoid(z)
    o_ref[...] = _dot(z.astype(BF16), pw2_ref[...])


def _conv_branch(yg, dw_w, dw_b, ln_g, ln_b, pw2, seq, ts):
    t, c = yg.shape
    kern = functools.partial(_conv_branch_kernel, ts=ts, tiles_per_seq=seq // ts)
    return pl.pallas_call(
        kern,
        grid=(t // ts,),
        in_specs=[
            pl.BlockSpec((ts, c), lambda i: (i, 0)),
            _resident(dw_w.shape),
            _resident((1, c)),
            _resident((1, c)),
            _resident((1, c)),
            _resident((c, c)),
        ],
        out_specs=pl.BlockSpec((ts, c), lambda i: (i, 0)),
        out_shape=jax.ShapeDtypeStruct((t, c), F32),
        scratch_shapes=[pltpu.VMEM((ts + CONV_HALO, c), F32),
                        pltpu.VMEM((SUBLANES - 1, ts + CONV_HALO, c), F32),
                        pltpu.VMEM((ts, c), F32)],
        compiler_params=_params(("arbitrary",)),
        name="conv_branch",
    )(yg, dw_w, dw_b, ln_g, ln_b, pw2)


DSA_TQ = 256
DSA_TK = 512
DSA_NACC = 4
ATT_NACC = 2
DSA_HG = 2
BISECT_LINEAR = 24
BISECT_UNROLL = 2
BISECT_MAX = 96


def _ordered_key(f):
    b = lax.bitcast_convert_type(f, jnp.int32)
    return b ^ ((b >> 31) & jnp.int32(0x7FFFFFFF))


def _from_ordered_key(k):
    return lax.bitcast_convert_type(k ^ ((k >> 31) & jnp.int32(0x7FFFFFFF)), F32)


def _dsa_kernel(q_ref, k_ref, vt_ref, qcat_ref, kcat_ref, wit_ref, tri_ref, o_ref,
                sc_ref, lg0_ref, lg1_ref, acc_ref, *, topk):
    lg_refs = (lg0_ref, lg1_ref)
    tq, tk = DSA_TQ, DSA_TK
    qt = pl.program_id(1)
    nck = ((qt + 1) * tq + tk - 1) // tk
    idx_scale = (IDX_HEADS ** -0.5) * (IDX_HEAD_DIM ** -0.5)
    kf = float(topk)

    q_pos = qt * tq + lax.broadcasted_iota(jnp.int32, (1, tq), 1)
    key_off = lax.broadcasted_iota(jnp.int32, (tk, 1), 0)

    def chunk_loop(body, init):
        def pair(j, carry):
            return body(2 * j + 1, body(2 * j, carry))
        carry = lax.fori_loop(0, lax.shift_right_logical(nck, 1), pair, init)
        return lax.cond((nck & 1) == 1, lambda cr: body(nck - 1, cr), lambda cr: cr, carry)

    def score_chunk(c, carry):
        mn, mx = carry
        r0 = pl.multiple_of(c * tk, tk)
        kc = kcat_ref[pl.ds(r0, tk), :]
        acc = jnp.zeros((tk, tq), F32)
        for h in range(IDX_HEADS):
            d = _dot_nt(kc, qcat_ref[:, h * IDX_CAT:(h + 1) * IDX_CAT])
            acc = acc + jnp.maximum(d, 0.0) * wit_ref[h:h + 1, :]
        s = acc * idx_scale
        causal = (c * tk + key_off) <= q_pos
        lo_s = jnp.where(causal, s, POS_INF)
        hi_s = jnp.where(causal, s, NEG_INF)
        sc_ref[c] = hi_s
        mn = jnp.minimum(mn, jnp.min(lo_s, axis=0, keepdims=True))
        mx = jnp.maximum(mx, jnp.max(hi_s, axis=0, keepdims=True))
        return mn, mx

    row_min, row_max = chunk_loop(
        score_chunk, (jnp.full((1, tq), POS_INF, F32), jnp.full((1, tq), NEG_INF, F32)))

    def count_rows(cand, strict):
        cb = jnp.broadcast_to(cand, (SUBLANES, tq))

        def body(c, parts):
            parts = list(parts)
            for i in range(tk // SUBLANES):
                s = sc_ref[c, i * SUBLANES:(i + 1) * SUBLANES, :]
                hit = (s > cb) if strict else (s >= cb)
                a = i % DSA_NACC
                parts[a] = jnp.where(hit, parts[a] + 1.0, parts[a])
            return tuple(parts)

        parts = lax.fori_loop(0, nck, body,
                              tuple(jnp.zeros((SUBLANES, tq), F32) for _ in range(DSA_NACC)))
        return jnp.sum(functools.reduce(jnp.add, parts), axis=0, keepdims=True)

    n_valid = (q_pos + 1).astype(F32)
    few = n_valid <= kf
    zero = jnp.zeros((1, tq), F32)
    cnt_ge0 = count_rows(zero, strict=False)
    cnt_gt0 = count_rows(zero, strict=True)
    pos = cnt_ge0 >= kf
    lo0 = jnp.where(pos, 0.0, row_min)
    cnt_lo0 = jnp.where(pos, cnt_ge0, n_valid)
    hi0 = jnp.where(pos, _from_ordered_key(_ordered_key(row_max) + 1), 0.0)

    def is_done(lo, hi, cnt_lo, done):
        adjacent = _ordered_key(hi) <= _ordered_key(lo) + 1
        return jnp.where((cnt_lo == kf) | adjacent, 1.0, done)

    def not_done(state):
        it, active = state[0], state[1]
        return jnp.logical_and(active > 0, it < BISECT_MAX)

    def step(it, lo, hi, cnt_lo, done):
        klo, khi = _ordered_key(lo), _ordered_key(hi)
        mid_bits = _from_ordered_key((klo >> 1) + (khi >> 1) + (klo & khi & 1))
        mid = jnp.where(it < BISECT_LINEAR, 0.5 * lo + 0.5 * hi, mid_bits)
        cnt = count_rows(mid, strict=False)
        ge = cnt >= kf
        upd = done < 0.5
        lo = jnp.where(upd & ge, mid, lo)
        cnt_lo = jnp.where(upd & ge, cnt, cnt_lo)
        hi = jnp.where(upd & jnp.logical_not(ge), mid, hi)
        return lo, hi, cnt_lo, is_done(lo, hi, cnt_lo, done)

    def bisect(state):
        it, _, lo, hi, cnt_lo, done = state
        for _ in range(BISECT_UNROLL):
            lo, hi, cnt_lo, done = step(it, lo, hi, cnt_lo, done)
        active = jnp.max(1.0 - done).astype(jnp.int32)
        return it + BISECT_UNROLL, active, lo, hi, cnt_lo, done

    done0 = jnp.where(few | ((cnt_gt0 < kf) & pos), 1.0, 0.0)
    done0 = is_done(lo0, hi0, cnt_lo0, done0)
    active0 = jnp.max(1.0 - done0).astype(jnp.int32)
    state = lax.while_loop(not_done, bisect,
                           (jnp.int32(0), active0, lo0, hi0, cnt_lo0, done0))
    thr = jnp.where(few, NEG_INF, state[2])

    cnt_gt = count_rows(thr, strict=True)
    need = jnp.where(few, 0.0, kf - cnt_gt)

    def bias_chunk(c, seen):
        s = sc_ref[c]
        eq = s == thr
        eqf = jnp.where(eq, 1.0, 0.0)
        before = _dot(tri_ref[...], eqf.astype(BF16)) + seen
        keep = (s > thr) | (eq & (before < need))
        sc_ref[c] = jnp.where(keep, 0.0, NEG_INF)
        return seen + jnp.sum(eqf, axis=0, keepdims=True)

    lax.fori_loop(0, nck, bias_chunk, jnp.zeros((1, tq), F32))

    def fold_rows(x, op, parts):
        parts = list(parts)
        for i in range(tk // SUBLANES):
            a = i % ATT_NACC
            parts[a] = op(parts[a], x[i * SUBLANES:(i + 1) * SUBLANES, :])
        return tuple(parts)

    def init_parts(value):
        return tuple(tuple(jnp.full((SUBLANES, tq), value, F32) for _ in range(ATT_NACC))
                     for _ in range(DSA_HG))

    def head_cols(grp, i):
        h = grp * DSA_HG + i
        return slice(h * ATT_HEAD_DIM, (h + 1) * ATT_HEAD_DIM)

    def make_body(grp_l, grp_p, m_rows):
        def body(c, carry):
            mparts, lparts = carry
            r0 = pl.multiple_of(c * tk, tk)
            if grp_l is not None:
                new_m = []
                for i in range(DSA_HG):
                    hs = head_cols(grp_l, i)
                    l = _dot_nt(k_ref[pl.ds(r0, tk), hs], q_ref[:, hs]) + sc_ref[c]
                    lg_refs[grp_l % 2][i, c] = l
                    new_m.append(fold_rows(l, jnp.maximum, mparts[i]))
                mparts = tuple(new_m)
            if grp_p is not None:
                new_l = []
                for i in range(DSA_HG):
                    hs = head_cols(grp_p, i)
                    p = jnp.exp2(lg_refs[grp_p % 2][i, c] - m_rows[i])
                    new_l.append(fold_rows(p, jnp.add, lparts[i]))
                    acc_ref[i] = acc_ref[i] + _dot(vt_ref[c, hs, :], p.astype(BF16))
                lparts = tuple(new_l)
            return mparts, lparts
        return body

    ngrp = ATT_HEADS // DSA_HG
    m_rows = None
    for phase in range(ngrp + 1):
        grp_l = phase if phase < ngrp else None
        grp_p = phase - 1 if phase > 0 else None
        if grp_p is not None:
            acc_ref[...] = jnp.zeros(acc_ref.shape, F32)
        mparts, lparts = chunk_loop(make_body(grp_l, grp_p, m_rows),
                                    (init_parts(NEG_INF), init_parts(0.0)))
        if grp_p is not None:
            for i in range(DSA_HG):
                lsum = jnp.sum(functools.reduce(jnp.add, lparts[i]), axis=0, keepdims=True)
                o_ref[:, head_cols(grp_p, i)] = (acc_ref[i] / lsum).T
        if grp_l is not None:
            m_rows = [jnp.max(functools.reduce(jnp.maximum, mp), axis=0, keepdims=True)
                      for mp in mparts]


def _dsa(qkm, vt, qcat, kcat, wit, tri, batch, seq):
    tq, tk = DSA_TQ, DSA_TK
    d = D_MODEL
    nq = seq // tq
    nc = seq // tk
    topk = min(TOPK_MAX, seq // 4)
    kern = functools.partial(_dsa_kernel, topk=topk)
    qrow = lambda b, i: (b * nq + i, 0)
    return pl.pallas_call(
        kern,
        grid=(batch, nq),
        in_specs=[
            pl.BlockSpec((tq, d), qrow),
            pl.BlockSpec((seq, d), lambda b, i: (b, 1), pipeline_mode=pl.Buffered(1)),
            pl.BlockSpec((nc, d, tk), lambda b, i: (b, 0, 0), pipeline_mode=pl.Buffered(1)),
            pl.BlockSpec((tq, qcat.shape[1]), qrow),
            pl.BlockSpec((seq, kcat.shape[1]), lambda b, i: (b, 0)),
            pl.BlockSpec((IDX_HEADS, tq), lambda b, i: (0, b * nq + i)),
            _resident((tk, tk)),
        ],
        out_specs=pl.BlockSpec((tq, d), qrow),
        out_shape=jax.ShapeDtypeStruct((batch * seq, d), F32),
        scratch_shapes=[
            pltpu.VMEM((nc, tk, tq), F32),
            pltpu.VMEM((DSA_HG, nc, tk, tq), F32),
            pltpu.VMEM((DSA_HG, nc, tk, tq), F32),
            pltpu.VMEM((DSA_HG, ATT_HEAD_DIM, tq), F32),
        ],
        compiler_params=_params(("arbitrary", "arbitrary")),
        name="dsa",
    )(qkm, qkm, vt, qcat, kcat, wit, tri)


def _mem_attn_kernel(qm_ref, mk_ref, mv_ref, o_ref):
    scale = MEM_HEAD_DIM ** -0.5
    for h in range(MEM_HEADS):
        hs = slice(h * MEM_HEAD_DIM, (h + 1) * MEM_HEAD_DIM)
        l = _dot_nt(qm_ref[:, hs], mk_ref[:, hs]) * scale
        p = jnp.exp(l - jnp.max(l, axis=1, keepdims=True))
        acc = _dot(p.astype(BF16), mv_ref[:, hs])
        o_ref[:, hs] = acc / jnp.sum(p, axis=1, keepdims=True)


def _mem_attn(qkm, mkv, batch, seq, n_mem, tm):
    d = D_MODEL
    nt = seq // tm
    return pl.pallas_call(
        _mem_attn_kernel,
        grid=(batch, nt),
        in_specs=[
            pl.BlockSpec((tm, d), lambda b, i: (b * nt + i, 2)),
            pl.BlockSpec((n_mem, d), lambda b, i: (b, 0)),
            pl.BlockSpec((n_mem, d), lambda b, i: (b, 1)),
        ],
        out_specs=pl.BlockSpec((tm, d), lambda b, i: (b * nt + i, 0)),
        out_shape=jax.ShapeDtypeStruct((batch * seq, d), F32),
        compiler_params=_params(("arbitrary", "arbitrary")),
        name="mem_attn",
    )(qkm, mkv, mkv)


def _merge_kernel(x_ref, yc_ref, ya_ref, ym_ref, g1_ref, wg_ref, bg_ref, wo_ref, gp_ref, o_ref):
    d = x_ref.shape[1]
    x = x_ref[...]
    h = _rms(x, g1_ref[...]).astype(BF16)
    merged = jnp.zeros(x.shape, F32)
    for br, y_ref in enumerate((yc_ref, ya_ref, ym_ref)):
        cs = slice(br * d, (br + 1) * d)
        gate = jax.nn.sigmoid(_dot(h, wg_ref[:, cs]) + bg_ref[:, cs])
        merged = merged + gate * y_ref[...]
    out = _dot(merged.astype(BF16), wo_ref[...])
    o_ref[...] = x + _rms(out, gp_ref[...])


def _merge(x2, yc, ya, ym, g1, w_gate, b_gate, w_out, g_post, tm):
    t, d = x2.shape
    row = lambda i: (i, 0)
    return pl.pallas_call(
        _merge_kernel,
        grid=(t // tm,),
        in_specs=[
            pl.BlockSpec((tm, d), row), pl.BlockSpec((tm, d), row),
            pl.BlockSpec((tm, d), row), pl.BlockSpec((tm, d), row),
            _resident((1, d)),
            _resident((d, 3 * d)),
            _resident((1, 3 * d)),
            _resident((d, d)),
            _resident((1, d)),
        ],
        out_specs=pl.BlockSpec((tm, d), row),
        out_shape=jax.ShapeDtypeStruct((t, d), F32),
        compiler_params=_params(("arbitrary",)),
        name="merge_out",
    )(x2, yc, ya, ym, g1, w_gate, b_gate, w_out, g_post)


FFN_FC = 2816


def _ffn_kernel(x_ref, g2_ref, wu_ref, cw_ref, cb_ref, wd_ref, gp_ref, o_ref,
                ext_ref, tail_ref, *, tm, tiles_per_seq):
    first = pl.program_id(0) % tiles_per_seq == 0
    halo = SUBLANES
    f = wd_ref.shape[0]
    fc = FFN_FC
    x = x_ref[...]
    h = _rms(x, g2_ref[...]).astype(BF16)

    @pl.when(pl.program_id(0) == 0)
    def _():
        tail_ref[...] = jnp.zeros(tail_ref.shape, F32)

    def conv_half(col0, slot, buf):
        cs = slice(col0, col0 + fc)
        up = _dot(h, wu_ref[:, cs])

        ext_ref[buf, 0:halo, :] = jnp.where(first, 0.0, tail_ref[slot])
        ext_ref[buf, halo:halo + tm, :] = up
        tail_ref[slot] = up[tm - halo:tm, :]
        u = cb_ref[:, cs] + cw_ref[2:3, cs] * up
        u = u + cw_ref[1:2, cs] * ext_ref[buf, halo - 1:halo - 1 + tm, :]
        u = u + cw_ref[0:1, cs] * ext_ref[buf, halo - 2:halo - 2 + tm, :]
        return u

    acc = None
    for j in range(f // fc):
        ug = conv_half(j * fc, 2 * j, 2 * (j % 2))
        uv = conv_half(f + j * fc, 2 * j + 1, 2 * (j % 2) + 1)
        act = (ug * jax.nn.sigmoid(ug)) * uv
        part = _dot(act.astype(BF16), wd_ref[j * fc:(j + 1) * fc, :])
        acc = part if acc is None else acc + part
    o_ref[...] = x + _rms(acc, gp_ref[...])


def _ffn(x1, g2, w_up, cw, cb, w_down, g_post, seq, tm):
    t, d = x1.shape
    f = w_down.shape[0]
    fc = FFN_FC
    kern = functools.partial(_ffn_kernel, tm=tm, tiles_per_seq=seq // tm)
    return pl.pallas_call(
        kern,
        grid=(t // tm,),
        in_specs=[
            pl.BlockSpec((tm, d), lambda i: (i, 0)),
            _resident((1, d)),
            _resident(w_up.shape),
            _resident(cw.shape),
            _resident(cb.shape),
            _resident(w_down.shape),
            _resident((1, d)),
        ],
        out_specs=pl.BlockSpec((tm, d), lambda i: (i, 0)),
        out_shape=jax.ShapeDtypeStruct((t, d), F32),
        scratch_shapes=[
            pltpu.VMEM((2 * min(2, f // fc), tm + SUBLANES, fc), F32),
            pltpu.VMEM((2 * (f // fc), SUBLANES, fc), F32),
        ],
        compiler_params=_params(("arbitrary",)),
        name="ffn",
    )(x1, g2, w_up, cw, cb, w_down, g_post)


def _pad_rows(a, rows):
    return jnp.pad(a, ((0, rows - a.shape[0]), (0, 0)))


def _pad_cols(a, cols):
    return jnp.pad(a, ((0, 0), (0, cols - a.shape[1])))


def _layer(x2, mem2, batch, seq, n_mem, p):
    d = D_MODEL
    row = lambda v: v.reshape(1, -1)
    w_in = p["w_in"]
    c2 = 2 * d
    o_q, o_qi = c2, c2 + 3 * d
    o_wi = o_qi + IDX_HEADS * IDX_HEAD_DIM
    o_ki = o_wi + IDX_HEADS
    o_qm = o_ki + IDX_HEAD_DIM

    w_glu = w_in[:, :c2].astype(BF16)
    w_qkm = jnp.concatenate([w_in[:, o_q:o_q + 2 * d], w_in[:, o_qm:]], axis=1).astype(BF16)
    w_vt = w_in[:, o_q + 2 * d:o_qi].T.astype(BF16)
    w_idx = jnp.concatenate(
        [w_in[:, o_qi:o_wi], _pad_cols(w_in[:, o_ki:o_qm], LANES),
         _pad_cols(w_in[:, o_wi:o_ki], LANES)], axis=1)
    w_idx_hi = w_idx.astype(BF16)
    w_idx_lo = (w_idx - w_idx_hi.astype(F32)).astype(BF16)

    g1 = row(p["norm1_pre_g"])
    yg, qkm, vt, qcat, kcat, wi = _proj(x2, g1, w_glu, w_qkm, w_vt, w_idx_hi, w_idx_lo)

    dw_w = _pad_rows(p["conv_dw_w"], 32)
    y_conv = _conv_branch(yg, dw_w, row(p["conv_dw_b"]), row(p["conv_ln_g"]),
                          row(p["conv_ln_b"]), p["conv_pw2"].astype(BF16), seq, ts=256)

    tk = DSA_TK
    tri = (lax.broadcasted_iota(jnp.int32, (tk, tk), 1)
           < lax.broadcasted_iota(jnp.int32, (tk, tk), 0)).astype(BF16)
    wit = wi[:, :IDX_HEADS].T
    y_att = _dsa(qkm, vt, qcat, kcat, wit, tri, batch, seq)

    mkv = _proj_bf16(mem2, row(p["mem_norm_g"]), p["w_mem_kv"].astype(BF16), tm=n_mem)
    y_mem = _mem_attn(qkm, mkv, batch, seq, n_mem, tm=512)

    x1 = _merge(x2, y_conv, y_att, y_mem, g1, p["w_gate"].astype(BF16), row(p["b_gate"]),
                p["w_out"].astype(BF16), row(p["norm1_post_g"]), tm=256)

    x2o = _ffn(x1, row(p["norm2_pre_g"]), p["w_up"].astype(BF16),
               _pad_rows(p["ffn_dw_w"], SUBLANES), row(p["ffn_dw_b"]),
               p["w_down"].astype(BF16), row(p["norm2_post_g"]), seq, tm=512)
    return x2o


def kernel(x, mem, norm1_pre_g, w_in, conv_dw_w, conv_dw_b, conv_ln_g, conv_ln_b, conv_pw2,
           mem_norm_g, w_mem_kv, w_gate, b_gate, w_out, norm1_post_g, norm2_pre_g, w_up,
           ffn_dw_w, ffn_dw_b, w_down, norm2_post_g):
    batch, seq, d = x.shape
    n_mem = mem.shape[1]
    assert d == D_MODEL and seq % DSA_TK == 0 and n_mem % SUBLANES == 0
    names = ("norm1_pre_g", "w_in", "conv_dw_w", "conv_dw_b", "conv_ln_g", "conv_ln_b",
             "conv_pw2", "mem_norm_g", "w_mem_kv", "w_gate", "b_gate", "w_out", "norm1_post_g",
             "norm2_pre_g", "w_up", "ffn_dw_w", "ffn_dw_b", "w_down", "norm2_post_g")
    vals = (norm1_pre_g, w_in, conv_dw_w, conv_dw_b, conv_ln_g, conv_ln_b, conv_pw2,
            mem_norm_g, w_mem_kv, w_gate, b_gate, w_out, norm1_post_g, norm2_pre_g, w_up,
            ffn_dw_w, ffn_dw_b, w_down, norm2_post_g)
    x2 = x.reshape(batch * seq, d)
    mem2 = mem.reshape(batch * n_mem, d)
    for l in range(norm1_pre_g.shape[0]):
        x2 = _layer(x2, mem2, batch, seq, n_mem, {n: v[l] for n, v in zip(names, vals)})
    return x2.reshape(batch, seq, d)
```

```python
import functools

import jax
import jax.numpy as jnp
from jax import lax
from jax.experimental import pallas as pl
from jax.experimental.pallas import tpu as pltpu

EPS = 1e-6
D_MODEL = 1024
CONV_K = 31
ATT_HEADS = 8
ATT_HEAD_DIM = 128
IDX_HEADS = 8
IDX_HEAD_DIM = 64
TOPK_MAX = 256
MEM_HEADS = 4
MEM_HEAD_DIM = 256
FFN_DIM = 2816
FFN_CONV_K = 3

LANES = 128
SUBLANES = 8
VMEM_LIMIT_BYTES = 56 * 1024 * 1024

F32 = jnp.float32
BF16 = jnp.bfloat16
NEG_INF = float("-inf")
POS_INF = float("inf")


def _params(semantics):
    return pltpu.CompilerParams(dimension_semantics=semantics,
                                vmem_limit_bytes=VMEM_LIMIT_BYTES)


def _resident(shape):
    return pl.BlockSpec(shape, lambda *_: (0,) * len(shape), pipeline_mode=pl.Buffered(1))


def _dot(a, b):
    return jnp.dot(a, b, preferred_element_type=F32)


def _dot_nt(a, b):
    return lax.dot_general(a, b, (((1,), (1,)), ((), ())), preferred_element_type=F32)


def _rms(x, g):
    return x * lax.rsqrt(jnp.mean(x * x, axis=-1, keepdims=True) + EPS) * g


def _split_bf16(v):
    hi = v.astype(BF16).astype(F32)
    lo = (v - hi).astype(BF16).astype(F32)
    return hi, lo


PROJ_TN = 512
Q_PRESCALE = ATT_HEAD_DIM ** -0.5 * 1.4426950408889634
IDX_CAT = 2 * LANES


def _proj_kernel(x_ref, g_ref, wglu_ref, wqkm_ref, wvt_ref, whi_ref, wlo_ref,
                 yg_ref, qkm_ref, vt_ref, qcat_ref, kcat_ref, wi_ref):
    h = _rms(x_ref[...], g_ref[...])
    h_hi = h.astype(BF16)
    h_lo = (h - h_hi.astype(F32)).astype(BF16)

    c = yg_ref.shape[1]
    for j in range(c // PROJ_TN):
        a = _dot(h_hi, wglu_ref[:, j * PROJ_TN:(j + 1) * PROJ_TN])
        gate = _dot(h_hi, wglu_ref[:, c + j * PROJ_TN:c + (j + 1) * PROJ_TN])
        yg_ref[:, j * PROJ_TN:(j + 1) * PROJ_TN] = a * jax.nn.sigmoid(gate)

    for j in range(qkm_ref.shape[1] // PROJ_TN):
        cs = slice(j * PROJ_TN, (j + 1) * PROJ_TN)
        r = _dot(h_hi, wqkm_ref[:, cs])
        if (j + 1) * PROJ_TN <= ATT_HEADS * ATT_HEAD_DIM:
            r = r * Q_PRESCALE
        qkm_ref[:, cs] = r.astype(BF16)

    for j in range(vt_ref.shape[1] // PROJ_TN):
        cs = slice(j * PROJ_TN, (j + 1) * PROJ_TN)
        vt_ref[0, cs, :] = _dot_nt(wvt_ref[cs, :], h_hi).astype(BF16)

    whi = whi_ref[...]
    out = _dot(h_hi, whi) + _dot(h_lo, whi) + _dot(h_hi, wlo_ref[...])
    low_half = lax.broadcasted_iota(jnp.int32, (1, LANES), 1) < IDX_HEAD_DIM
    npair = IDX_HEADS // 2
    for p in range(npair):
        hi, lo = _split_bf16(out[:, p * LANES:(p + 1) * LANES])
        hi_sw = pltpu.roll(hi, IDX_HEAD_DIM, 1)
        lo_sw = pltpu.roll(lo, IDX_HEAD_DIM, 1)
        base = 2 * p * IDX_CAT
        qcat_ref[:, base:base + LANES] = jnp.where(low_half, hi, lo_sw).astype(BF16)
        qcat_ref[:, base + LANES:base + 2 * LANES] = jnp.where(low_half, hi, 0.0).astype(BF16)
        qcat_ref[:, base + 2 * LANES:base + 3 * LANES] = jnp.where(low_half, hi_sw, lo).astype(BF16)
        qcat_ref[:, base + 3 * LANES:base + 4 * LANES] = jnp.where(low_half, hi_sw, 0.0).astype(BF16)
    kw = out[:, npair * LANES:(npair + 1) * LANES]
    k_hi, k_lo = _split_bf16(kw)
    kcat_ref[:, 0:LANES] = jnp.where(low_half, k_hi, pltpu.roll(k_hi, IDX_HEAD_DIM, 1)).astype(BF16)
    kcat_ref[:, LANES:2 * LANES] = jnp.where(low_half, k_lo, 0.0).astype(BF16)
    wi_ref[...] = pltpu.roll(kw, IDX_HEAD_DIM, 1)


def _proj(x2, g, w_glu, w_qkm, w_vt, w_hi, w_lo):
    tm = DSA_TK
    t, d = x2.shape
    c = w_glu.shape[1] // 2
    nq = w_qkm.shape[1]
    nv = w_vt.shape[0]
    row = lambda i: (i, 0)
    return pl.pallas_call(
        _proj_kernel,
        grid=(t // tm,),
        in_specs=[
            pl.BlockSpec((tm, d), row),
            _resident((1, d)),
            _resident(w_glu.shape),
            _resident(w_qkm.shape),
            _resident(w_vt.shape),
            _resident(w_hi.shape),
            _resident(w_lo.shape),
        ],
        out_specs=[
            pl.BlockSpec((tm, c), row),
            pl.BlockSpec((tm, nq), row),
            pl.BlockSpec((1, nv, tm), lambda i: (i, 0, 0)),
            pl.BlockSpec((tm, IDX_HEADS * IDX_CAT), row),
            pl.BlockSpec((tm, IDX_CAT), row),
            pl.BlockSpec((tm, LANES), row),
        ],
        out_shape=[
            jax.ShapeDtypeStruct((t, c), F32),
            jax.ShapeDtypeStruct((t, nq), BF16),
            jax.ShapeDtypeStruct((t // tm, nv, tm), BF16),
            jax.ShapeDtypeStruct((t, IDX_HEADS * IDX_CAT), BF16),
            jax.ShapeDtypeStruct((t, IDX_CAT), BF16),
            jax.ShapeDtypeStruct((t, LANES), F32),
        ],
        compiler_params=_params(("arbitrary",)),
        name="proj",
    )(x2, g, w_glu, w_qkm, w_vt, w_hi, w_lo)


def _proj_bf16_kernel(x_ref, g_ref, w_ref, o_ref):
    h = _rms(x_ref[...], g_ref[...]).astype(BF16)
    o_ref[...] = _dot(h, w_ref[...]).astype(BF16)


def _proj_bf16(x2, g, w, tm):
    t, d = x2.shape
    n = w.shape[1]
    return pl.pallas_call(
        _proj_bf16_kernel,
        grid=(t // tm,),
        in_specs=[
            pl.BlockSpec((tm, d), lambda i: (i, 0)),
            _resident((1, d)),
            _resident(w.shape),
        ],
        out_specs=pl.BlockSpec((tm, n), lambda i: (i, 0)),
        out_shape=jax.ShapeDtypeStruct((t, n), BF16),
        compiler_params=_params(("arbitrary",)),
        name="proj_mem",
    )(x2, g, w)


CONV_HALO = 32
CONV_ROWS = 64


def _conv_branch_kernel(y_ref, dww_ref, dwb_ref, lng_ref, lnb_ref, pw2_ref, o_ref,
                        ext_ref, cv_ref, *, ts, tiles_per_seq):
    c = y_ref.shape[1]
    first = pl.program_id(0) % tiles_per_seq == 0

    @pl.when(first)
    def _():
        ext_ref[:, 0:CONV_HALO, :] = jnp.zeros((c // LANES, CONV_HALO, LANES), F32)

    @pl.when(jnp.logical_not(first))
    def _():
        ext_ref[:, 0:CONV_HALO, :] = ext_ref[:, ts:ts + CONV_HALO, :]

    off0 = CONV_HALO - (CONV_K - 1)
    for cg in range(c // LANES):
        cs = slice(cg * LANES, (cg + 1) * LANES)
        ext_ref[cg, CONV_HALO:CONV_HALO + ts, :] = y_ref[:, cs]
        for rb in range(ts // CONV_ROWS):
            r0 = rb * CONV_ROWS
            acc = jnp.broadcast_to(dwb_ref[:, cs], (CONV_ROWS, LANES))
            for j in range(CONV_K):
                a0 = r0 + off0 + j
                acc = acc + dww_ref[j:j + 1, cs] * ext_ref[cg, a0:a0 + CONV_ROWS, :]
            cv_ref[r0:r0 + CONV_ROWS, cs] = acc

    y = cv_ref[...]
    mu = jnp.mean(y, axis=-1, keepdims=True)
    yc = y - mu
    var = jnp.mean(yc * yc, axis=-1, keepdims=True)
    z = yc * lax.rsqrt(var + EPS) * lng_ref[...] + lnb_ref[...]
    z = z * jax.nn.sigmoid(z)
    o_ref[...] = _dot(z.astype(BF16), pw2_ref[...])


def _conv_branch(yg, dw_w, dw_b, ln_g, ln_b, pw2, seq, ts):
    t, c = yg.shape
    kern = functools.partial(_conv_branch_kernel, ts=ts, tiles_per_seq=seq // ts)
    return pl.pallas_call(
        kern,
        grid=(t // ts,),
        in_specs=[
            pl.BlockSpec((ts, c), lambda i: (i, 0)),
            _resident(dw_w.shape),
            _resident((1, c)),
            _resident((1, c)),
            _resident((1, c)),
            _resident((c, c)),
        ],
        out_specs=pl.BlockSpec((ts, c), lambda i: (i, 0)),
        out_shape=jax.ShapeDtypeStruct((t, c), F32),
        scratch_shapes=[pltpu.VMEM((c // LANES, ts + CONV_HALO, LANES), F32),
                        pltpu.VMEM((ts, c), F32)],
        compiler_params=_params(("arbitrary",)),
        name="conv_branch",
    )(yg, dw_w, dw_b, ln_g, ln_b, pw2)


DSA_TQ = 256
DSA_TK = 512
DSA_NACC = 4
ATT_NACC = 2
DSA_HG = 2
BISECT_LINEAR = 24
BISECT_UNROLL = 2
BISECT_MAX = 40


def _ordered_key(f):
    b = lax.bitcast_convert_type(f, jnp.int32)
    return b ^ ((b >> 31) & jnp.int32(0x7FFFFFFF))


def _from_ordered_key(k):
    return lax.bitcast_convert_type(k ^ ((k >> 31) & jnp.int32(0x7FFFFFFF)), F32)


def _dsa_kernel(q_ref, k_ref, vt_ref, qcat_ref, kcat_ref, wit_ref, tri_ref, o_ref,
                sc_ref, lg0_ref, lg1_ref, acc_ref, *, topk):
    lg_refs = (lg0_ref, lg1_ref)
    tq, tk = DSA_TQ, DSA_TK
    qt = pl.program_id(1)
    nck = ((qt + 1) * tq + tk - 1) // tk
    idx_scale = (IDX_HEADS ** -0.5) * (IDX_HEAD_DIM ** -0.5)
    kf = float(topk)

    q_pos = qt * tq + lax.broadcasted_iota(jnp.int32, (1, tq), 1)
    key_off = lax.broadcasted_iota(jnp.int32, (tk, 1), 0)

    def chunk_loop(body, init):
        def pair(j, carry):
            return body(2 * j + 1, body(2 * j, carry))
        carry = lax.fori_loop(0, lax.shift_right_logical(nck, 1), pair, init)
        return lax.cond((nck & 1) == 1, lambda cr: body(nck - 1, cr), lambda cr: cr, carry)

    def score_chunk(c, carry):
        mn, mx = carry
        r0 = pl.multiple_of(c * tk, tk)
        kc = kcat_ref[pl.ds(r0, tk), :]
        acc = jnp.zeros((tk, tq), F32)
        for h in range(IDX_HEADS):
            d = _dot_nt(kc, qcat_ref[:, h * IDX_CAT:(h + 1) * IDX_CAT])
            acc = acc + jnp.maximum(d, 0.0) * wit_ref[h:h + 1, :]
        s = acc * idx_scale
        causal = (c * tk + key_off) <= q_pos
        lo_s = jnp.where(causal, s, POS_INF)
        hi_s = jnp.where(causal, s, NEG_INF)
        sc_ref[c] = hi_s
        mn = jnp.minimum(mn, jnp.min(lo_s, axis=0, keepdims=True))
        mx = jnp.maximum(mx, jnp.max(hi_s, axis=0, keepdims=True))
        return mn, mx

    row_min, row_max = chunk_loop(
        score_chunk, (jnp.full((1, tq), POS_INF, F32), jnp.full((1, tq), NEG_INF, F32)))

    def count_rows(cand, strict):
        cb = jnp.broadcast_to(cand, (SUBLANES, tq))

        def body(c, parts):
            parts = list(parts)
            for i in range(tk // SUBLANES):
                s = sc_ref[c, i * SUBLANES:(i + 1) * SUBLANES, :]
                hit = (s > cb) if strict else (s >= cb)
                a = i % DSA_NACC
                parts[a] = jnp.where(hit, parts[a] + 1.0, parts[a])
            return tuple(parts)

        parts = lax.fori_loop(0, nck, body,
                              tuple(jnp.zeros((SUBLANES, tq), F32) for _ in range(DSA_NACC)))
        return jnp.sum(functools.reduce(jnp.add, parts), axis=0, keepdims=True)

    n_valid = (q_pos + 1).astype(F32)
    few = n_valid <= kf
    zero = jnp.zeros((1, tq), F32)
    cnt_ge0 = count_rows(zero, strict=False)
    cnt_gt0 = count_rows(zero, strict=True)
    pos = cnt_ge0 >= kf
    lo0 = jnp.where(pos, 0.0, row_min)
    cnt_lo0 = jnp.where(pos, cnt_ge0, n_valid)
    hi0 = jnp.where(pos, _from_ordered_key(_ordered_key(row_max) + 1), 0.0)

    def narrow(mid, cnt, lo, hi, cnt_lo, done):
        ge = cnt >= kf
        upd = done < 0.5
        return (jnp.where(upd & ge, mid, lo), jnp.where(upd & jnp.logical_not(ge), mid, hi),
                jnp.where(upd & ge, cnt, cnt_lo))

    def any_active(done):
        return jnp.max(1.0 - done).astype(jnp.int32)

    settled = jnp.where(few | ((cnt_gt0 < kf) & pos), 1.0, 0.0)

    def lin_done(lo, hi, mid, cnt_lo, done):
        return jnp.where((cnt_lo == kf) | (mid == lo) | (mid == hi), 1.0, done)

    def lin_trip(state):
        it, _, lo, hi, mid, cnt_lo, done = state
        for _ in range(BISECT_UNROLL):
            cnt = count_rows(mid, strict=False)
            lo, hi, cnt_lo = narrow(mid, cnt, lo, hi, cnt_lo, done)
            mid = 0.5 * lo + 0.5 * hi
            done = lin_done(lo, hi, mid, cnt_lo, done)
        return it + BISECT_UNROLL, any_active(done), lo, hi, mid, cnt_lo, done

    mid0 = 0.5 * lo0 + 0.5 * hi0
    done0 = lin_done(lo0, hi0, mid0, cnt_lo0, settled)
    _, _, lo1, hi1, _, cnt_lo1, _ = lax.while_loop(
        lambda st: jnp.logical_and(st[1] > 0, st[0] < BISECT_LINEAR), lin_trip,
        (jnp.int32(0), any_active(done0), lo0, hi0, mid0, cnt_lo0, done0))

    def lat_done(lo, hi, cnt_lo, done):
        adjacent = _ordered_key(hi) <= _ordered_key(lo) + 1
        return jnp.where((cnt_lo == kf) | adjacent, 1.0, done)

    def lat_trip(state):
        it, _, lo, hi, cnt_lo, done = state
        klo, khi = _ordered_key(lo), _ordered_key(hi)
        mid = _from_ordered_key((klo >> 1) + (khi >> 1) + (klo & khi & 1))
        cnt = count_rows(mid, strict=False)
        lo, hi, cnt_lo = narrow(mid, cnt, lo, hi, cnt_lo, done)
        done = lat_done(lo, hi, cnt_lo, done)
        return it + 1, any_active(done), lo, hi, cnt_lo, done

    done1 = lat_done(lo1, hi1, cnt_lo1, settled)
    state = lax.while_loop(
        lambda st: jnp.logical_and(st[1] > 0, st[0] < BISECT_MAX), lat_trip,
        (jnp.int32(0), any_active(done1), lo1, hi1, cnt_lo1, done1))
    thr = jnp.where(few, NEG_INF, state[2])

    cnt_gt = count_rows(thr, strict=True)
    need = jnp.where(few, 0.0, kf - cnt_gt)

    def bias_chunk(c, seen):
        s = sc_ref[c]
        eq = s == thr
        eqf = jnp.where(eq, 1.0, 0.0)
        before = _dot(tri_ref[...], eqf.astype(BF16)) + seen
        keep = (s > thr) | (eq & (before < need))
        sc_ref[c] = jnp.where(keep, 0.0, NEG_INF)
        return seen + jnp.sum(eqf, axis=0, keepdims=True)

    lax.fori_loop(0, nck, bias_chunk, jnp.zeros((1, tq), F32))

    def fold_rows(x, op, parts):
        parts = list(parts)
        for i in range(tk // SUBLANES):
            a = i % ATT_NACC
            parts[a] = op(parts[a], x[i * SUBLANES:(i + 1) * SUBLANES, :])
        return tuple(parts)

    def init_parts(value):
        return tuple(tuple(jnp.full((SUBLANES, tq), value, F32) for _ in range(ATT_NACC))
                     for _ in range(DSA_HG))

    def head_cols(grp, i):
        h = grp * DSA_HG + i
        return slice(h * ATT_HEAD_DIM, (h + 1) * ATT_HEAD_DIM)

    def make_body(grp_l, grp_p, m_rows):
        def body(c, carry):
            mparts, lparts = carry
            r0 = pl.multiple_of(c * tk, tk)
            if grp_l is not None:
                new_m = []
                for i in range(DSA_HG):
                    hs = head_cols(grp_l, i)
                    l = _dot_nt(k_ref[pl.ds(r0, tk), hs], q_ref[:, hs]) + sc_ref[c]
                    lg_refs[grp_l % 2][i, c] = l
                    new_m.append(fold_rows(l, jnp.maximum, mparts[i]))
                mparts = tuple(new_m)
            if grp_p is not None:
                new_l = []
                for i in range(DSA_HG):
                    hs = head_cols(grp_p, i)
                    p = jnp.exp2(lg_refs[grp_p % 2][i, c] - m_rows[i])
                    new_l.append(fold_rows(p, jnp.add, lparts[i]))
                    acc_ref[i] = acc_ref[i] + _dot(vt_ref[c, hs, :], p.astype(BF16))
                lparts = tuple(new_l)
            return mparts, lparts
        return body

    ngrp = ATT_HEADS // DSA_HG
    m_rows = None
    for phase in range(ngrp + 1):
        grp_l = phase if phase < ngrp else None
        grp_p = phase - 1 if phase > 0 else None
        if grp_p is not None:
            acc_ref[...] = jnp.zeros(acc_ref.shape, F32)
        mparts, lparts = chunk_loop(make_body(grp_l, grp_p, m_rows),
                                    (init_parts(NEG_INF), init_parts(0.0)))
        if grp_p is not None:
            for i in range(DSA_HG):
                lsum = jnp.sum(functools.reduce(jnp.add, lparts[i]), axis=0, keepdims=True)
                o_ref[:, head_cols(grp_p, i)] = (acc_ref[i] / lsum).T
        if grp_l is not None:
            m_rows = [jnp.max(functools.reduce(jnp.maximum, mp), axis=0, keepdims=True)
                      for mp in mparts]


def _dsa(qkm, vt, qcat, kcat, wit, tri, batch, seq):
    tq, tk = DSA_TQ, DSA_TK
    d = D_MODEL
    nq = seq // tq
    nc = seq // tk
    topk = min(TOPK_MAX, seq // 4)
    kern = functools.partial(_dsa_kernel, topk=topk)
    qrow = lambda b, i: (b * nq + i, 0)
    return pl.pallas_call(
        kern,
        grid=(batch, nq),
        in_specs=[
            pl.BlockSpec((tq, d), qrow),
            pl.BlockSpec((seq, d), lambda b, i: (b, 1), pipeline_mode=pl.Buffered(1)),
            pl.BlockSpec((nc, d, tk), lambda b, i: (b, 0, 0), pipeline_mode=pl.Buffered(1)),
            pl.BlockSpec((tq, qcat.shape[1]), qrow),
            pl.BlockSpec((seq, kcat.shape[1]), lambda b, i: (b, 0)),
            pl.BlockSpec((IDX_HEADS, tq), lambda b, i: (0, b * nq + i)),
            _resident((tk, tk)),
        ],
        out_specs=pl.BlockSpec((tq, d), qrow),
        out_shape=jax.ShapeDtypeStruct((batch * seq, d), F32),
        scratch_shapes=[
            pltpu.VMEM((nc, tk, tq), F32),
            pltpu.VMEM((DSA_HG, nc, tk, tq), F32),
            pltpu.VMEM((DSA_HG, nc, tk, tq), F32),
            pltpu.VMEM((DSA_HG, ATT_HEAD_DIM, tq), F32),
        ],
        compiler_params=_params(("arbitrary", "arbitrary")),
        name="dsa",
    )(qkm, qkm, vt, qcat, kcat, wit, tri)


def _mem_attn_kernel(qm_ref, mk_ref, mv_ref, o_ref):
    scale = MEM_HEAD_DIM ** -0.5
    for h in range(MEM_HEADS):
        hs = slice(h * MEM_HEAD_DIM, (h + 1) * MEM_HEAD_DIM)
        l = _dot_nt(qm_ref[:, hs], mk_ref[:, hs]) * scale
        p = jnp.exp(l - jnp.max(l, axis=1, keepdims=True))
        acc = _dot(p.astype(BF16), mv_ref[:, hs])
        o_ref[:, hs] = acc / jnp.sum(p, axis=1, keepdims=True)


def _mem_attn(qkm, mkv, batch, seq, n_mem, tm):
    d = D_MODEL
    nt = seq // tm
    return pl.pallas_call(
        _mem_attn_kernel,
        grid=(batch, nt),
        in_specs=[
            pl.BlockSpec((tm, d), lambda b, i: (b * nt + i, 2)),
            pl.BlockSpec((n_mem, d), lambda b, i: (b, 0)),
            pl.BlockSpec((n_mem, d), lambda b, i: (b, 1)),
        ],
        out_specs=pl.BlockSpec((tm, d), lambda b, i: (b * nt + i, 0)),
        out_shape=jax.ShapeDtypeStruct((batch * seq, d), F32),
        compiler_params=_params(("arbitrary", "arbitrary")),
        name="mem_attn",
    )(qkm, mkv, mkv)


def _merge_kernel(x_ref, yc_ref, ya_ref, ym_ref, g1_ref, wg_ref, bg_ref, wo_ref, gp_ref, o_ref):
    d = x_ref.shape[1]
    x = x_ref[...]
    h = _rms(x, g1_ref[...]).astype(BF16)
    merged = jnp.zeros(x.shape, F32)
    for br, y_ref in enumerate((yc_ref, ya_ref, ym_ref)):
        cs = slice(br * d, (br + 1) * d)
        gate = jax.nn.sigmoid(_dot(h, wg_ref[:, cs]) + bg_ref[:, cs])
        merged = merged + gate * y_ref[...]
    out = _dot(merged.astype(BF16), wo_ref[...])
    o_ref[...] = x + _rms(out, gp_ref[...])


def _merge(x2, yc, ya, ym, g1, w_gate, b_gate, w_out, g_post, tm):
    t, d = x2.shape
    row = lambda i: (i, 0)
    return pl.pallas_call(
        _merge_kernel,
        grid=(t // tm,),
        in_specs=[
            pl.BlockSpec((tm, d), row), pl.BlockSpec((tm, d), row),
            pl.BlockSpec((tm, d), row), pl.BlockSpec((tm, d), row),
            _resident((1, d)),
            _resident((d, 3 * d)),
            _resident((1, 3 * d)),
            _resident((d, d)),
            _resident((1, d)),
        ],
        out_specs=pl.BlockSpec((tm, d), row),
        out_shape=jax.ShapeDtypeStruct((t, d), F32),
        compiler_params=_params(("arbitrary",)),
        name="merge_out",
    )(x2, yc, ya, ym, g1, w_gate, b_gate, w_out, g_post)


FFN_FC = 2816


def _ffn_kernel(x_ref, g2_ref, wu_ref, cw_ref, cb_ref, wd_ref, gp_ref, o_ref,
                ext_ref, tail_ref, *, tm, tiles_per_seq):
    first = pl.program_id(0) % tiles_per_seq == 0
    halo = SUBLANES
    f = wd_ref.shape[0]
    fc = FFN_FC
    x = x_ref[...]
    h = _rms(x, g2_ref[...]).astype(BF16)

    @pl.when(pl.program_id(0) == 0)
    def _():
        tail_ref[...] = jnp.zeros(tail_ref.shape, F32)

    def conv_half(col0, slab0):
        up = _dot(h, wu_ref[:, col0:col0 + fc])
        outs = []
        for s in range(fc // LANES):
            piece = up[:, s * LANES:(s + 1) * LANES]
            cs = slice(col0 + s * LANES, col0 + (s + 1) * LANES)
            slab = slab0 + s
            ext_ref[slab, 0:halo, :] = jnp.where(first, 0.0, tail_ref[slab])
            ext_ref[slab, halo:halo + tm, :] = piece
            tail_ref[slab] = piece[tm - halo:tm, :]
            u = cb_ref[:, cs] + cw_ref[2:3, cs] * piece
            u = u + cw_ref[1:2, cs] * ext_ref[slab, halo - 1:halo - 1 + tm, :]
            u = u + cw_ref[0:1, cs] * ext_ref[slab, halo - 2:halo - 2 + tm, :]
            outs.append(u)
        return jnp.concatenate(outs, axis=1)

    acc = None
    nslab = fc // LANES
    for j in range(f // fc):
        ug = conv_half(j * fc, 2 * j * nslab)
        uv = conv_half(f + j * fc, (2 * j + 1) * nslab)
        act = (ug * jax.nn.sigmoid(ug)) * uv
        part = _dot(act.astype(BF16), wd_ref[j * fc:(j + 1) * fc, :])
        acc = part if acc is None else acc + part
    o_ref[...] = x + _rms(acc, gp_ref[...])


def _ffn(x1, g2, w_up, cw, cb, w_down, g_post, seq, tm):
    t, d = x1.shape
    f = w_down.shape[0]
    fc = FFN_FC
    kern = functools.partial(_ffn_kernel, tm=tm, tiles_per_seq=seq // tm)
    return pl.pallas_call(
        kern,
        grid=(t // tm,),
        in_specs=[
            pl.BlockSpec((tm, d), lambda i: (i, 0)),
            _resident((1, d)),
            _resident(w_up.shape),
            _resident(cw.shape),
            _resident(cb.shape),
            _resident(w_down.shape),
            _resident((1, d)),
        ],
        out_specs=pl.BlockSpec((tm, d), lambda i: (i, 0)),
        out_shape=jax.ShapeDtypeStruct((t, d), F32),
        scratch_shapes=[
            pltpu.VMEM((2 * f // LANES, tm + SUBLANES, LANES), F32),
            pltpu.VMEM((2 * f // LANES, SUBLANES, LANES), F32),
        ],
        compiler_params=_params(("arbitrary",)),
        name="ffn",
    )(x1, g2, w_up, cw, cb, w_down, g_post)


def _pad_rows(a, rows):
    return jnp.pad(a, ((0, rows - a.shape[0]), (0, 0)))


def _pad_cols(a, cols):
    return jnp.pad(a, ((0, 0), (0, cols - a.shape[1])))


def _layer(x2, mem2, batch, seq, n_mem, p):
    d = D_MODEL
    row = lambda v: v.reshape(1, -1)
    w_in = p["w_in"]
    c2 = 2 * d
    o_q, o_qi = c2, c2 + 3 * d
    o_wi = o_qi + IDX_HEADS * IDX_HEAD_DIM
    o_ki = o_wi + IDX_HEADS
    o_qm = o_ki + IDX_HEAD_DIM

    w_glu = w_in[:, :c2].astype(BF16)
    w_qkm = jnp.concatenate([w_in[:, o_q:o_q + 2 * d], w_in[:, o_qm:]], axis=1).astype(BF16)
    w_vt = w_in[:, o_q + 2 * d:o_qi].T.astype(BF16)
    w_idx = jnp.concatenate(
        [w_in[:, o_qi:o_wi],
         _pad_cols(jnp.concatenate([w_in[:, o_ki:o_qm], w_in[:, o_wi:o_ki]], axis=1), LANES)], axis=1)
    w_idx_hi = w_idx.astype(BF16)
    w_idx_lo = (w_idx - w_idx_hi.astype(F32)).astype(BF16)

    g1 = row(p["norm1_pre_g"])
    yg, qkm, vt, qcat, kcat, wi = _proj(x2, g1, w_glu, w_qkm, w_vt, w_idx_hi, w_idx_lo)

    dw_w = _pad_rows(p["conv_dw_w"], 32)
    y_conv = _conv_branch(yg, dw_w, row(p["conv_dw_b"]), row(p["conv_ln_g"]),
                          row(p["conv_ln_b"]), p["conv_pw2"].astype(BF16), seq, ts=512)

    tk = DSA_TK
    tri = (lax.broadcasted_iota(jnp.int32, (tk, tk), 1)
           < lax.broadcasted_iota(jnp.int32, (tk, tk), 0)).astype(BF16)
    wit = wi[:, :IDX_HEADS].T
    y_att = _dsa(qkm, vt, qcat, kcat, wit, tri, batch, seq)

    mkv = _proj_bf16(mem2, row(p["mem_norm_g"]), p["w_mem_kv"].astype(BF16), tm=n_mem)
    y_mem = _mem_attn(qkm, mkv, batch, seq, n_mem, tm=512)

    x1 = _merge(x2, y_conv, y_att, y_mem, g1, p["w_gate"].astype(BF16), row(p["b_gate"]),
                p["w_out"].astype(BF16), row(p["norm1_post_g"]), tm=256)

    x2o = _ffn(x1, row(p["norm2_pre_g"]), p["w_up"].astype(BF16),
               _pad_rows(p["ffn_dw_w"], SUBLANES), row(p["ffn_dw_b"]),
               p["w_down"].astype(BF16), row(p["norm2_post_g"]), seq, tm=512)
    return x2o


def kernel(x, mem, norm1_pre_g, w_in, conv_dw_w, conv_dw_b, conv_ln_g, conv_ln_b, conv_pw2,
           mem_norm_g, w_mem_kv, w_gate, b_gate, w_out, norm1_post_g, norm2_pre_g, w_up,
           ffn_dw_w, ffn_dw_b, w_down, norm2_post_g):
    batch, seq, d = x.shape
    n_mem = mem.shape[1]
    assert d == D_MODEL and seq % DSA_TK == 0 and n_mem % SUBLANES == 0
    names = ("norm1_pre_g", "w_in", "conv_dw_w", "conv_dw_b", "conv_ln_g", "conv_ln_b",
             "conv_pw2", "mem_norm_g", "w_mem_kv", "w_gate", "b_gate", "w_out", "norm1_post_g",
             "norm2_pre_g", "w_up", "ffn_dw_w", "ffn_dw_b", "w_down", "norm2_post_g")
    vals = (norm1_pre_g, w_in, conv_dw_w, conv_dw_b, conv_ln_g, conv_ln_b, conv_pw2,
            mem_norm_g, w_mem_kv, w_gate, b_gate, w_out, norm1_post_g, norm2_pre_g, w_up,
            ffn_dw_w, ffn_dw_b, w_down, norm2_post_g)
    x2 = x.reshape(batch * seq, d)
    mem2 = mem.reshape(batch * n_mem, d)
    for l in range(norm1_pre_g.shape[0]):
        x2 = _layer(x2, mem2, batch, seq, n_mem, {n: v[l] for n, v in zip(names, vals)})
    return x2.reshape(batch, seq, d)
```

```python
import functools

import jax
import jax.numpy as jnp
from jax import lax
from jax.experimental import pallas as pl
from jax.experimental.pallas import tpu as pltpu

EPS = 1e-6
D_MODEL = 1024
CONV_K = 31
ATT_HEADS = 8
ATT_HEAD_DIM = 128
IDX_HEADS = 8
IDX_HEAD_DIM = 64
TOPK_MAX = 256
MEM_HEADS = 4
MEM_HEAD_DIM = 256
FFN_DIM = 2816
FFN_CONV_K = 3

LANES = 128
SUBLANES = 8
VMEM_LIMIT_BYTES = 56 * 1024 * 1024

F32 = jnp.float32
BF16 = jnp.bfloat16
NEG_INF = float("-inf")
POS_INF = float("inf")


def _params(semantics):
    return pltpu.CompilerParams(dimension_semantics=semantics,
                                vmem_limit_bytes=VMEM_LIMIT_BYTES)


def _resident(shape):
    return pl.BlockSpec(shape, lambda *_: (0,) * len(shape), pipeline_mode=pl.Buffered(1))


def _dot(a, b):
    return jnp.dot(a, b, preferred_element_type=F32)


def _dot_nt(a, b):
    return lax.dot_general(a, b, (((1,), (1,)), ((), ())), preferred_element_type=F32)


def _rms(x, g):
    return x * lax.rsqrt(jnp.mean(x * x, axis=-1, keepdims=True) + EPS) * g


def _split_bf16(v):
    hi = v.astype(BF16).astype(F32)
    lo = (v - hi).astype(BF16).astype(F32)
    return hi, lo


PROJ_TN = 512
Q_PRESCALE = ATT_HEAD_DIM ** -0.5 * 1.4426950408889634
IDX_CAT = 2 * LANES


def _proj_kernel(x_ref, g_ref, wglu_ref, wqkm_ref, wvt_ref, whi_ref, wlo_ref,
                 yg_ref, qkm_ref, vt_ref, qcat_ref, kcat_ref, wi_ref):
    h = _rms(x_ref[...], g_ref[...])
    h_hi = h.astype(BF16)
    h_lo = (h - h_hi.astype(F32)).astype(BF16)

    c = yg_ref.shape[1]
    for j in range(c // PROJ_TN):
        a = _dot(h_hi, wglu_ref[:, j * PROJ_TN:(j + 1) * PROJ_TN])
        gate = _dot(h_hi, wglu_ref[:, c + j * PROJ_TN:c + (j + 1) * PROJ_TN])
        yg_ref[:, j * PROJ_TN:(j + 1) * PROJ_TN] = a * jax.nn.sigmoid(gate)

    for j in range(qkm_ref.shape[1] // PROJ_TN):
        cs = slice(j * PROJ_TN, (j + 1) * PROJ_TN)
        r = _dot(h_hi, wqkm_ref[:, cs])
        if (j + 1) * PROJ_TN <= ATT_HEADS * ATT_HEAD_DIM:
            r = r * Q_PRESCALE
        qkm_ref[:, cs] = r.astype(BF16)

    for j in range(vt_ref.shape[1] // PROJ_TN):
        cs = slice(j * PROJ_TN, (j + 1) * PROJ_TN)
        vt_ref[0, cs, :] = _dot_nt(wvt_ref[cs, :], h_hi).astype(BF16)

    whi = whi_ref[...]
    out = _dot(h_hi, whi) + _dot(h_lo, whi) + _dot(h_hi, wlo_ref[...])
    low_half = lax.broadcasted_iota(jnp.int32, (1, LANES), 1) < IDX_HEAD_DIM
    npair = IDX_HEADS // 2
    for p in range(npair):
        hi, lo = _split_bf16(out[:, p * LANES:(p + 1) * LANES])
        hi_sw = pltpu.roll(hi, IDX_HEAD_DIM, 1)
        lo_sw = pltpu.roll(lo, IDX_HEAD_DIM, 1)
        base = 2 * p * IDX_CAT
        qcat_ref[:, base:base + LANES] = jnp.where(low_half, hi, lo_sw).astype(BF16)
        qcat_ref[:, base + LANES:base + 2 * LANES] = jnp.where(low_half, hi, 0.0).astype(BF16)
        qcat_ref[:, base + 2 * LANES:base + 3 * LANES] = jnp.where(low_half, hi_sw, lo).astype(BF16)
        qcat_ref[:, base + 3 * LANES:base + 4 * LANES] = jnp.where(low_half, hi_sw, 0.0).astype(BF16)
    kw = out[:, npair * LANES:(npair + 1) * LANES]
    k_hi, k_lo = _split_bf16(kw)
    kcat_ref[:, 0:LANES] = jnp.where(low_half, k_hi, pltpu.roll(k_hi, IDX_HEAD_DIM, 1)).astype(BF16)
    kcat_ref[:, LANES:2 * LANES] = jnp.where(low_half, k_lo, 0.0).astype(BF16)
    wi_ref[...] = pltpu.roll(kw, IDX_HEAD_DIM, 1)


def _proj(x2, g, w_glu, w_qkm, w_vt, w_hi, w_lo):
    tm = DSA_TK
    t, d = x2.shape
    c = w_glu.shape[1] // 2
    nq = w_qkm.shape[1]
    nv = w_vt.shape[0]
    row = lambda i: (i, 0)
    return pl.pallas_call(
        _proj_kernel,
        grid=(t // tm,),
        in_specs=[
            pl.BlockSpec((tm, d), row),
            _resident((1, d)),
            _resident(w_glu.shape),
            _resident(w_qkm.shape),
            _resident(w_vt.shape),
            _resident(w_hi.shape),
            _resident(w_lo.shape),
        ],
        out_specs=[
            pl.BlockSpec((tm, c), row),
            pl.BlockSpec((tm, nq), row),
            pl.BlockSpec((1, nv, tm), lambda i: (i, 0, 0)),
            pl.BlockSpec((tm, IDX_HEADS * IDX_CAT), row),
            pl.BlockSpec((tm, IDX_CAT), row),
            pl.BlockSpec((tm, LANES), row),
        ],
        out_shape=[
            jax.ShapeDtypeStruct((t, c), F32),
            jax.ShapeDtypeStruct((t, nq), BF16),
            jax.ShapeDtypeStruct((t // tm, nv, tm), BF16),
            jax.ShapeDtypeStruct((t, IDX_HEADS * IDX_CAT), BF16),
            jax.ShapeDtypeStruct((t, IDX_CAT), BF16),
            jax.ShapeDtypeStruct((t, LANES), F32),
        ],
        compiler_params=_params(("arbitrary",)),
        name="proj",
    )(x2, g, w_glu, w_qkm, w_vt, w_hi, w_lo)


def _proj_bf16_kernel(x_ref, g_ref, w_ref, o_ref):
    h = _rms(x_ref[...], g_ref[...]).astype(BF16)
    o_ref[...] = _dot(h, w_ref[...]).astype(BF16)


def _proj_bf16(x2, g, w, tm):
    t, d = x2.shape
    n = w.shape[1]
    return pl.pallas_call(
        _proj_bf16_kernel,
        grid=(t // tm,),
        in_specs=[
            pl.BlockSpec((tm, d), lambda i: (i, 0)),
            _resident((1, d)),
            _resident(w.shape),
        ],
        out_specs=pl.BlockSpec((tm, n), lambda i: (i, 0)),
        out_shape=jax.ShapeDtypeStruct((t, n), BF16),
        compiler_params=_params(("arbitrary",)),
        name="proj_mem",
    )(x2, g, w)


CONV_HALO = 32
CONV_ROWS = 64


def _conv_branch_kernel(y_ref, dww_ref, dwb_ref, lng_ref, lnb_ref, pw2_ref, o_ref,
                        ext_ref, cv_ref, *, ts, tiles_per_seq):
    c = y_ref.shape[1]
    first = pl.program_id(0) % tiles_per_seq == 0

    @pl.when(first)
    def _():
        ext_ref[:, 0:CONV_HALO, :] = jnp.zeros((c // LANES, CONV_HALO, LANES), F32)

    @pl.when(jnp.logical_not(first))
    def _():
        ext_ref[:, 0:CONV_HALO, :] = ext_ref[:, ts:ts + CONV_HALO, :]

    off0 = CONV_HALO - (CONV_K - 1)
    for cg in range(c // LANES):
        cs = slice(cg * LANES, (cg + 1) * LANES)
        ext_ref[cg, CONV_HALO:CONV_HALO + ts, :] = y_ref[:, cs]
        for rb in range(ts // CONV_ROWS):
            r0 = rb * CONV_ROWS
            acc = jnp.broadcast_to(dwb_ref[:, cs], (CONV_ROWS, LANES))
            for j in range(CONV_K):
                a0 = r0 + off0 + j
                acc = acc + dww_ref[j:j + 1, cs] * ext_ref[cg, a0:a0 + CONV_ROWS, :]
            cv_ref[r0:r0 + CONV_ROWS, cs] = acc

    y = cv_ref[...]
    mu = jnp.mean(y, axis=-1, keepdims=True)
    yc = y - mu
    var = jnp.mean(yc * yc, axis=-1, keepdims=True)
    z = yc * lax.rsqrt(var + EPS) * lng_ref[...] + lnb_ref[...]
    z = z * jax.nn.sigmoid(z)
    o_ref[...] = _dot(z.astype(BF16), pw2_ref[...])


def _conv_branch(yg, dw_w, dw_b, ln_g, ln_b, pw2, seq, ts):
    t, c = yg.shape
    kern = functools.partial(_conv_branch_kernel, ts=ts, tiles_per_seq=seq // ts)
    return pl.pallas_call(
        kern,
        grid=(t // ts,),
        in_specs=[
            pl.BlockSpec((ts, c), lambda i: (i, 0)),
            _resident(dw_w.shape),
            _resident((1, c)),
            _resident((1, c)),
            _resident((1, c)),
            _resident((c, c)),
        ],
        out_specs=pl.BlockSpec((ts, c), lambda i: (i, 0)),
        out_shape=jax.ShapeDtypeStruct((t, c), F32),
        scratch_shapes=[pltpu.VMEM((c // LANES, ts + CONV_HALO, LANES), F32),
                        pltpu.VMEM((ts, c), F32)],
        compiler_params=_params(("arbitrary",)),
        name="conv_branch",
    )(yg, dw_w, dw_b, ln_g, ln_b, pw2)


DSA_TQ = 256
DSA_TK = 512
DSA_NACC = 4
ATT_NACC = 2
DSA_HG = 2
BISECT_LINEAR = 24
BISECT_UNROLL = 2
BISECT_MAX = 40


def _ordered_key(f):
    b = lax.bitcast_convert_type(f, jnp.int32)
    return b ^ ((b >> 31) & jnp.int32(0x7FFFFFFF))


def _from_ordered_key(k):
    return lax.bitcast_convert_type(k ^ ((k >> 31) & jnp.int32(0x7FFFFFFF)), F32)


def _dsa_kernel(q_ref, k_ref, vt_ref, qcat_ref, kcat_ref, wit_ref, tri_ref, o_ref,
                sc_ref, lg0_ref, lg1_ref, acc_ref, *, topk):
    lg_refs = (lg0_ref, lg1_ref)
    tq, tk = DSA_TQ, DSA_TK
    qt = pl.program_id(1)
    nck = ((qt + 1) * tq + tk - 1) // tk
    idx_scale = (IDX_HEADS ** -0.5) * (IDX_HEAD_DIM ** -0.5)
    kf = float(topk)

    q_pos = qt * tq + lax.broadcasted_iota(jnp.int32, (1, tq), 1)
    key_off = lax.broadcasted_iota(jnp.int32, (tk, 1), 0)

    def chunk_loop(body, init):
        def pair(j, carry):
            return body(2 * j + 1, body(2 * j, carry))
        carry = lax.fori_loop(0, lax.shift_right_logical(nck, 1), pair, init)
        return lax.cond((nck & 1) == 1, lambda cr: body(nck - 1, cr), lambda cr: cr, carry)

    def score_chunk(c, carry):
        mn, mx = carry
        r0 = pl.multiple_of(c * tk, tk)
        kc = kcat_ref[pl.ds(r0, tk), :]
        acc = jnp.zeros((tk, tq), F32)
        for h in range(IDX_HEADS):
            d = _dot_nt(kc, qcat_ref[:, h * IDX_CAT:(h + 1) * IDX_CAT])
            acc = acc + jnp.maximum(d, 0.0) * wit_ref[h:h + 1, :]
        s = acc * idx_scale
        causal = (c * tk + key_off) <= q_pos
        lo_s = jnp.where(causal, s, POS_INF)
        hi_s = jnp.where(causal, s, NEG_INF)
        sc_ref[c] = hi_s
        mn = jnp.minimum(mn, jnp.min(lo_s, axis=0, keepdims=True))
        mx = jnp.maximum(mx, jnp.max(hi_s, axis=0, keepdims=True))
        return mn, mx

    row_min, row_max = chunk_loop(
        score_chunk, (jnp.full((1, tq), POS_INF, F32), jnp.full((1, tq), NEG_INF, F32)))

    def count_rows(cand, strict):
        cb = jnp.broadcast_to(cand, (SUBLANES, tq))

        def body(c, parts):
            parts = list(parts)
            for i in range(tk // SUBLANES):
                s = sc_ref[c, i * SUBLANES:(i + 1) * SUBLANES, :]
                hit = (s > cb) if strict else (s >= cb)
                a = i % DSA_NACC
                parts[a] = jnp.where(hit, parts[a] + 1.0, parts[a])
            return tuple(parts)

        parts = lax.fori_loop(0, nck, body,
                              tuple(jnp.zeros((SUBLANES, tq), F32) for _ in range(DSA_NACC)))
        return jnp.sum(functools.reduce(jnp.add, parts), axis=0, keepdims=True)

    n_valid = (q_pos + 1).astype(F32)
    few = n_valid <= kf
    zero = jnp.zeros((1, tq), F32)
    cnt_ge0 = count_rows(zero, strict=False)
    cnt_gt0 = count_rows(zero, strict=True)
    pos = cnt_ge0 >= kf
    lo0 = jnp.where(pos, 0.0, row_min)
    cnt_lo0 = jnp.where(pos, cnt_ge0, n_valid)
    hi0 = jnp.where(pos, _from_ordered_key(_ordered_key(row_max) + 1), 0.0)
    cnt_hi0 = jnp.where(pos, 0.0, cnt_ge0)

    def narrow(mid, cnt, lo, hi, cnt_lo, cnt_hi, done):
        ge = (cnt >= kf) & (done < 0.5)
        lt = (cnt < kf) & (done < 0.5)
        return (jnp.where(ge, mid, lo), jnp.where(lt, mid, hi),
                jnp.where(ge, cnt, cnt_lo), jnp.where(lt, cnt, cnt_hi))

    def any_active(done):
        return jnp.max(1.0 - done).astype(jnp.int32)

    zero_thr = (cnt_gt0 < kf) & pos
    settled = jnp.where(few | zero_thr, 1.0, 0.0)

    def pinned(cnt_lo, cnt_hi):
        return (cnt_lo == kf) | (cnt_hi == kf - 1.0)

    def lin_done(lo, hi, mid, cnt_lo, cnt_hi, done):
        return jnp.where(pinned(cnt_lo, cnt_hi) | (mid == lo) | (mid == hi), 1.0, done)

    def lin_trip(state):
        it, _, lo, hi, mid, cnt_lo, cnt_hi, done = state
        for _ in range(BISECT_UNROLL):
            cnt = count_rows(mid, strict=False)
            lo, hi, cnt_lo, cnt_hi = narrow(mid, cnt, lo, hi, cnt_lo, cnt_hi, done)
            mid = 0.5 * lo + 0.5 * hi
            done = lin_done(lo, hi, mid, cnt_lo, cnt_hi, done)
        return it + BISECT_UNROLL, any_active(done), lo, hi, mid, cnt_lo, cnt_hi, done

    mid0 = 0.5 * lo0 + 0.5 * hi0
    done0 = lin_done(lo0, hi0, mid0, cnt_lo0, cnt_hi0, settled)
    _, _, lo1, hi1, _, cnt_lo1, cnt_hi1, _ = lax.while_loop(
        lambda st: jnp.logical_and(st[1] > 0, st[0] < BISECT_LINEAR), lin_trip,
        (jnp.int32(0), any_active(done0), lo0, hi0, mid0, cnt_lo0, cnt_hi0, done0))

    def lat_done(lo, hi, cnt_lo, cnt_hi, done):
        adjacent = _ordered_key(hi) <= _ordered_key(lo) + 1
        return jnp.where(pinned(cnt_lo, cnt_hi) | adjacent, 1.0, done)

    def lat_trip(state):
        it, _, lo, hi, cnt_lo, cnt_hi, done = state
        klo, khi = _ordered_key(lo), _ordered_key(hi)
        mid = _from_ordered_key((klo >> 1) + (khi >> 1) + (klo & khi & 1))
        cnt = count_rows(mid, strict=False)
        lo, hi, cnt_lo, cnt_hi = narrow(mid, cnt, lo, hi, cnt_lo, cnt_hi, done)
        done = lat_done(lo, hi, cnt_lo, cnt_hi, done)
        return it + 1, any_active(done), lo, hi, cnt_lo, cnt_hi, done

    done1 = lat_done(lo1, hi1, cnt_lo1, cnt_hi1, settled)
    _, _, lo2, hi2, cnt_lo2, cnt_hi2, _ = lax.while_loop(
        lambda st: jnp.logical_and(st[1] > 0, st[0] < BISECT_MAX), lat_trip,
        (jnp.int32(0), any_active(done1), lo1, hi1, cnt_lo1, cnt_hi1, done1))

    def max_below(bound):
        bb = jnp.broadcast_to(bound, (SUBLANES, tq))

        def body(c, parts):
            parts = list(parts)
            for i in range(tk // SUBLANES):
                s = sc_ref[c, i * SUBLANES:(i + 1) * SUBLANES, :]
                a = i % DSA_NACC
                parts[a] = jnp.maximum(parts[a], jnp.where(s < bb, s, NEG_INF))
            return tuple(parts)

        parts = lax.fori_loop(0, nck, body,
                              tuple(jnp.full((SUBLANES, tq), NEG_INF, F32) for _ in range(DSA_NACC)))
        return jnp.max(functools.reduce(jnp.maximum, parts), axis=0, keepdims=True)

    top_below = max_below(hi2)
    by_lo = cnt_lo2 == kf
    by_hi = jnp.logical_not(by_lo) & (cnt_hi2 == kf - 1.0)
    thr = jnp.where(by_hi, top_below, lo2)
    need = jnp.where(by_lo, kf, kf - cnt_hi2)
    thr = jnp.where(few, NEG_INF, jnp.where(zero_thr, 0.0, thr))
    need = jnp.where(few, 0.0, jnp.where(zero_thr, kf - cnt_gt0, need))

    def bias_chunk(c, seen):
        s = sc_ref[c]
        eq = s == thr
        eqf = jnp.where(eq, 1.0, 0.0)
        before = _dot(tri_ref[...], eqf.astype(BF16)) + seen
        keep = (s > thr) | (eq & (before < need))
        sc_ref[c] = jnp.where(keep, 0.0, NEG_INF)
        return seen + jnp.sum(eqf, axis=0, keepdims=True)

    lax.fori_loop(0, nck, bias_chunk, jnp.zeros((1, tq), F32))

    def fold_rows(x, op, parts):
        parts = list(parts)
        for i in range(tk // SUBLANES):
            a = i % ATT_NACC
            parts[a] = op(parts[a], x[i * SUBLANES:(i + 1) * SUBLANES, :])
        return tuple(parts)

    def init_parts(value):
        return tuple(tuple(jnp.full((SUBLANES, tq), value, F32) for _ in range(ATT_NACC))
                     for _ in range(DSA_HG))

    def head_cols(grp, i):
        h = grp * DSA_HG + i
        return slice(h * ATT_HEAD_DIM, (h + 1) * ATT_HEAD_DIM)

    def make_body(grp_l, grp_p, m_rows):
        def body(c, carry):
            mparts, lparts = carry
            r0 = pl.multiple_of(c * tk, tk)
            if grp_l is not None:
                new_m = []
                for i in range(DSA_HG):
                    hs = head_cols(grp_l, i)
                    l = _dot_nt(k_ref[pl.ds(r0, tk), hs], q_ref[:, hs]) + sc_ref[c]
                    lg_refs[grp_l % 2][i, c] = l
                    new_m.append(fold_rows(l, jnp.maximum, mparts[i]))
                mparts = tuple(new_m)
            if grp_p is not None:
                new_l = []
                for i in range(DSA_HG):
                    hs = head_cols(grp_p, i)
                    p = jnp.exp2(lg_refs[grp_p % 2][i, c] - m_rows[i])
                    new_l.append(fold_rows(p, jnp.add, lparts[i]))
                    acc_ref[i] = acc_ref[i] + _dot(vt_ref[c, hs, :], p.astype(BF16))
                lparts = tuple(new_l)
            return mparts, lparts
        return body

    ngrp = ATT_HEADS // DSA_HG
    m_rows = None
    for phase in range(ngrp + 1):
        grp_l = phase if phase < ngrp else None
        grp_p = phase - 1 if phase > 0 else None
        if grp_p is not None:
            acc_ref[...] = jnp.zeros(acc_ref.shape, F32)
        mparts, lparts = chunk_loop(make_body(grp_l, grp_p, m_rows),
                                    (init_parts(NEG_INF), init_parts(0.0)))
        if grp_p is not None:
            for i in range(DSA_HG):
                lsum = jnp.sum(functools.reduce(jnp.add, lparts[i]), axis=0, keepdims=True)
                o_ref[:, head_cols(grp_p, i)] = (acc_ref[i] / lsum).T
        if grp_l is not None:
            m_rows = [jnp.max(functools.reduce(jnp.maximum, mp), axis=0, keepdims=True)
                      for mp in mparts]


def _dsa(qkm, vt, qcat, kcat, wit, tri, batch, seq):
    tq, tk = DSA_TQ, DSA_TK
    d = D_MODEL
    nq = seq // tq
    nc = seq // tk
    topk = min(TOPK_MAX, seq // 4)
    kern = functools.partial(_dsa_kernel, topk=topk)
    qrow = lambda b, i: (b * nq + i, 0)
    return pl.pallas_call(
        kern,
        grid=(batch, nq),
        in_specs=[
            pl.BlockSpec((tq, d), qrow),
            pl.BlockSpec((seq, d), lambda b, i: (b, 1), pipeline_mode=pl.Buffered(1)),
            pl.BlockSpec((nc, d, tk), lambda b, i: (b, 0, 0), pipeline_mode=pl.Buffered(1)),
            pl.BlockSpec((tq, qcat.shape[1]), qrow),
            pl.BlockSpec((seq, kcat.shape[1]), lambda b, i: (b, 0)),
            pl.BlockSpec((IDX_HEADS, tq), lambda b, i: (0, b * nq + i)),
            _resident((tk, tk)),
        ],
        out_specs=pl.BlockSpec((tq, d), qrow),
        out_shape=jax.ShapeDtypeStruct((batch * seq, d), F32),
        scratch_shapes=[
            pltpu.VMEM((nc, tk, tq), F32),
            pltpu.VMEM((DSA_HG, nc, tk, tq), F32),
            pltpu.VMEM((DSA_HG, nc, tk, tq), F32),
            pltpu.VMEM((DSA_HG, ATT_HEAD_DIM, tq), F32),
        ],
        compiler_params=_params(("arbitrary", "arbitrary")),
        name="dsa",
    )(qkm, qkm, vt, qcat, kcat, wit, tri)


def _mem_attn_kernel(qm_ref, mk_ref, mv_ref, o_ref):
    scale = MEM_HEAD_DIM ** -0.5
    for h in range(MEM_HEADS):
        hs = slice(h * MEM_HEAD_DIM, (h + 1) * MEM_HEAD_DIM)
        l = _dot_nt(qm_ref[:, hs], mk_ref[:, hs]) * scale
        p = jnp.exp(l - jnp.max(l, axis=1, keepdims=True))
        acc = _dot(p.astype(BF16), mv_ref[:, hs])
        o_ref[:, hs] = acc / jnp.sum(p, axis=1, keepdims=True)


def _mem_attn(qkm, mkv, batch, seq, n_mem, tm):
    d = D_MODEL
    nt = seq // tm
    return pl.pallas_call(
        _mem_attn_kernel,
        grid=(batch, nt),
        in_specs=[
            pl.BlockSpec((tm, d), lambda b, i: (b * nt + i, 2)),
            pl.BlockSpec((n_mem, d), lambda b, i: (b, 0)),
            pl.BlockSpec((n_mem, d), lambda b, i: (b, 1)),
        ],
        out_specs=pl.BlockSpec((tm, d), lambda b, i: (b * nt + i, 0)),
        out_shape=jax.ShapeDtypeStruct((batch * seq, d), F32),
        compiler_params=_params(("arbitrary", "arbitrary")),
        name="mem_attn",
    )(qkm, mkv, mkv)


def _merge_kernel(x_ref, yc_ref, ya_ref, ym_ref, g1_ref, wg_ref, bg_ref, wo_ref, gp_ref, o_ref):
    d = x_ref.shape[1]
    x = x_ref[...]
    h = _rms(x, g1_ref[...]).astype(BF16)
    merged = jnp.zeros(x.shape, F32)
    for br, y_ref in enumerate((yc_ref, ya_ref, ym_ref)):
        cs = slice(br * d, (br + 1) * d)
        gate = jax.nn.sigmoid(_dot(h, wg_ref[:, cs]) + bg_ref[:, cs])
        merged = merged + gate * y_ref[...]
    out = _dot(merged.astype(BF16), wo_ref[...])
    o_ref[...] = x + _rms(out, gp_ref[...])


def _merge(x2, yc, ya, ym, g1, w_gate, b_gate, w_out, g_post, tm):
    t, d = x2.shape
    row = lambda i: (i, 0)
    return pl.pallas_call(
        _merge_kernel,
        grid=(t // tm,),
        in_specs=[
            pl.BlockSpec((tm, d), row), pl.BlockSpec((tm, d), row),
            pl.BlockSpec((tm, d), row), pl.BlockSpec((tm, d), row),
            _resident((1, d)),
            _resident((d, 3 * d)),
            _resident((1, 3 * d)),
            _resident((d, d)),
            _resident((1, d)),
        ],
        out_specs=pl.BlockSpec((tm, d), row),
        out_shape=jax.ShapeDtypeStruct((t, d), F32),
        compiler_params=_params(("arbitrary",)),
        name="merge_out",
    )(x2, yc, ya, ym, g1, w_gate, b_gate, w_out, g_post)


FFN_FC = 2816


def _ffn_kernel(x_ref, g2_ref, wu_ref, cw_ref, cb_ref, wd_ref, gp_ref, o_ref,
                ext_ref, tail_ref, *, tm, tiles_per_seq):
    first = pl.program_id(0) % tiles_per_seq == 0
    halo = SUBLANES
    f = wd_ref.shape[0]
    fc = FFN_FC
    x = x_ref[...]
    h = _rms(x, g2_ref[...]).astype(BF16)

    @pl.when(pl.program_id(0) == 0)
    def _():
        tail_ref[...] = jnp.zeros(tail_ref.shape, F32)

    def conv_half(col0, slab0):
        up = _dot(h, wu_ref[:, col0:col0 + fc])
        outs = []
        for s in range(fc // LANES):
            piece = up[:, s * LANES:(s + 1) * LANES]
            cs = slice(col0 + s * LANES, col0 + (s + 1) * LANES)
            slab = slab0 + s
            ext_ref[slab, 0:halo, :] = jnp.where(first, 0.0, tail_ref[slab])
            ext_ref[slab, halo:halo + tm, :] = piece
            tail_ref[slab] = piece[tm - halo:tm, :]
            u = cb_ref[:, cs] + cw_ref[2:3, cs] * piece
            u = u + cw_ref[1:2, cs] * ext_ref[slab, halo - 1:halo - 1 + tm, :]
            u = u + cw_ref[0:1, cs] * ext_ref[slab, halo - 2:halo - 2 + tm, :]
            outs.append(u)
        return jnp.concatenate(outs, axis=1)

    acc = None
    nslab = fc // LANES
    for j in range(f // fc):
        ug = conv_half(j * fc, 2 * j * nslab)
        uv = conv_half(f + j * fc, (2 * j + 1) * nslab)
        act = (ug * jax.nn.sigmoid(ug)) * uv
        part = _dot(act.astype(BF16), wd_ref[j * fc:(j + 1) * fc, :])
        acc = part if acc is None else acc + part
    o_ref[...] = x + _rms(acc, gp_ref[...])


def _ffn(x1, g2, w_up, cw, cb, w_down, g_post, seq, tm):
    t, d = x1.shape
    f = w_down.shape[0]
    fc = FFN_FC
    kern = functools.partial(_ffn_kernel, tm=tm, tiles_per_seq=seq // tm)
    return pl.pallas_call(
        kern,
        grid=(t // tm,),
        in_specs=[
            pl.BlockSpec((tm, d), lambda i: (i, 0)),
            _resident((1, d)),
            _resident(w_up.shape),
            _resident(cw.shape),
            _resident(cb.shape),
            _resident(w_down.shape),
            _resident((1, d)),
        ],
        out_specs=pl.BlockSpec((tm, d), lambda i: (i, 0)),
        out_shape=jax.ShapeDtypeStruct((t, d), F32),
        scratch_shapes=[
            pltpu.VMEM((2 * f // LANES, tm + SUBLANES, LANES), F32),
            pltpu.VMEM((2 * f // LANES, SUBLANES, LANES), F32),
        ],
        compiler_params=_params(("arbitrary",)),
        name="ffn",
    )(x1, g2, w_up, cw, cb, w_down, g_post)


def _pad_rows(a, rows):
    return jnp.pad(a, ((0, rows - a.shape[0]), (0, 0)))


def _pad_cols(a, cols):
    return jnp.pad(a, ((0, 0), (0, cols - a.shape[1])))


def _layer(x2, mem2, batch, seq, n_mem, p):
    d = D_MODEL
    row = lambda v: v.reshape(1, -1)
    w_in = p["w_in"]
    c2 = 2 * d
    o_q, o_qi = c2, c2 + 3 * d
    o_wi = o_qi + IDX_HEADS * IDX_HEAD_DIM
    o_ki = o_wi + IDX_HEADS
    o_qm = o_ki + IDX_HEAD_DIM

    w_glu = w_in[:, :c2].astype(BF16)
    w_qkm = jnp.concatenate([w_in[:, o_q:o_q + 2 * d], w_in[:, o_qm:]], axis=1).astype(BF16)
    w_vt = w_in[:, o_q + 2 * d:o_qi].T.astype(BF16)
    w_idx = jnp.concatenate(
        [w_in[:, o_qi:o_wi],
         _pad_cols(jnp.concatenate([w_in[:, o_ki:o_qm], w_in[:, o_wi:o_ki]], axis=1), LANES)], axis=1)
    w_idx_hi = w_idx.astype(BF16)
    w_idx_lo = (w_idx - w_idx_hi.astype(F32)).astype(BF16)

    g1 = row(p["norm1_pre_g"])
    yg, qkm, vt, qcat, kcat, wi = _proj(x2, g1, w_glu, w_qkm, w_vt, w_idx_hi, w_idx_lo)

    dw_w = _pad_rows(p["conv_dw_w"], 32)
    y_conv = _conv_branch(yg, dw_w, row(p["conv_dw_b"]), row(p["conv_ln_g"]),
                          row(p["conv_ln_b"]), p["conv_pw2"].astype(BF16), seq, ts=512)

    tk = DSA_TK
    tri = (lax.broadcasted_iota(jnp.int32, (tk, tk), 1)
           < lax.broadcasted_iota(jnp.int32, (tk, tk), 0)).astype(BF16)
    wit = wi[:, :IDX_HEADS].T
    y_att = _dsa(qkm, vt, qcat, kcat, wit, tri, batch, seq)

    mkv = _proj_bf16(mem2, row(p["mem_norm_g"]), p["w_mem_kv"].astype(BF16), tm=n_mem)
    y_mem = _mem_attn(qkm, mkv, batch, seq, n_mem, tm=512)

    x1 = _merge(x2, y_conv, y_att, y_mem, g1, p["w_gate"].astype(BF16), row(p["b_gate"]),
                p["w_out"].astype(BF16), row(p["norm1_post_g"]), tm=256)

    x2o = _ffn(x1, row(p["norm2_pre_g"]), p["w_up"].astype(BF16),
               _pad_rows(p["ffn_dw_w"], SUBLANES), row(p["ffn_dw_b"]),
               p["w_down"].astype(BF16), row(p["norm2_post_g"]), seq, tm=512)
    return x2o


def kernel(x, mem, norm1_pre_g, w_in, conv_dw_w, conv_dw_b, conv_ln_g, conv_ln_b, conv_pw2,
           mem_norm_g, w_mem_kv, w_gate, b_gate, w_out, norm1_post_g, norm2_pre_g, w_up,
           ffn_dw_w, ffn_dw_b, w_down, norm2_post_g):
    batch, seq, d = x.shape
    n_mem = mem.shape[1]
    assert d == D_MODEL and seq % DSA_TK == 0 and n_mem % SUBLANES == 0
    names = ("norm1_pre_g", "w_in", "conv_dw_w", "conv_dw_b", "conv_ln_g", "conv_ln_b",
             "conv_pw2", "mem_norm_g", "w_mem_kv", "w_gate", "b_gate", "w_out", "norm1_post_g",
             "norm2_pre_g", "w_up", "ffn_dw_w", "ffn_dw_b", "w_down", "norm2_post_g")
    vals = (norm1_pre_g, w_in, conv_dw_w, conv_dw_b, conv_ln_g, conv_ln_b, conv_pw2,
            mem_norm_g, w_mem_kv, w_gate, b_gate, w_out, norm1_post_g, norm2_pre_g, w_up,
            ffn_dw_w, ffn_dw_b, w_down, norm2_post_g)
    x2 = x.reshape(batch * seq, d)
    mem2 = mem.reshape(batch * n_mem, d)
    for l in range(norm1_pre_g.shape[0]):
        x2 = _layer(x2, mem2, batch, seq, n_mem, {n: v[l] for n, v in zip(names, vals)})
    return x2.reshape(batch, seq, d)
```

```python
import functools

import jax
import jax.numpy as jnp
from jax import lax
from jax.experimental import pallas as pl
from jax.experimental.pallas import tpu as pltpu

EPS = 1e-6
D_MODEL = 1024
CONV_K = 31
ATT_HEADS = 8
ATT_HEAD_DIM = 128
IDX_HEADS = 8
IDX_HEAD_DIM = 64
TOPK_MAX = 256
MEM_HEADS = 4
MEM_HEAD_DIM = 256

LANES = 128
SUBLANES = 8
VMEM_LIMIT_BYTES = 56 * 1024 * 1024

F32 = jnp.float32
BF16 = jnp.bfloat16
NEG_INF = float("-inf")
POS_INF = float("inf")


def _params(semantics):
    return pltpu.CompilerParams(dimension_semantics=semantics,
                                vmem_limit_bytes=VMEM_LIMIT_BYTES)


def _resident(shape):
    return pl.BlockSpec(shape, lambda *_: (0,) * len(shape), pipeline_mode=pl.Buffered(1))


def _dot(a, b):
    return jnp.dot(a, b, preferred_element_type=F32)


def _dot_nt(a, b):
    return lax.dot_general(a, b, (((1,), (1,)), ((), ())), preferred_element_type=F32)


def _rms(x, g):
    return x * lax.rsqrt(jnp.mean(x * x, axis=-1, keepdims=True) + EPS) * g


def _split_bf16(v):
    hi = v.astype(BF16).astype(F32)
    lo = (v - hi).astype(BF16).astype(F32)
    return hi, lo


PROJ_TN = 512
Q_PRESCALE = ATT_HEAD_DIM ** -0.5 * 1.4426950408889634
IDX_CAT = 2 * LANES


def _proj_kernel(x_ref, g_ref, wglu_ref, wqkm_ref, wvt_ref, whi_ref, wlo_ref,
                 yg_ref, qkm_ref, vt_ref, qcat_ref, kcat_ref, wi_ref):
    h = _rms(x_ref[...], g_ref[...])
    h_hi = h.astype(BF16)
    h_lo = (h - h_hi.astype(F32)).astype(BF16)

    c = yg_ref.shape[1]
    for j in range(c // PROJ_TN):
        a = _dot(h_hi, wglu_ref[:, j * PROJ_TN:(j + 1) * PROJ_TN])
        gate = _dot(h_hi, wglu_ref[:, c + j * PROJ_TN:c + (j + 1) * PROJ_TN])
        yg_ref[:, j * PROJ_TN:(j + 1) * PROJ_TN] = a * jax.nn.sigmoid(gate)

    for j in range(qkm_ref.shape[1] // PROJ_TN):
        cs = slice(j * PROJ_TN, (j + 1) * PROJ_TN)
        r = _dot(h_hi, wqkm_ref[:, cs])
        if (j + 1) * PROJ_TN <= ATT_HEADS * ATT_HEAD_DIM:
            r = r * Q_PRESCALE
        qkm_ref[:, cs] = r.astype(BF16)

    for j in range(vt_ref.shape[1] // PROJ_TN):
        cs = slice(j * PROJ_TN, (j + 1) * PROJ_TN)
        vt_ref[0, cs, :] = _dot_nt(wvt_ref[cs, :], h_hi).astype(BF16)

    whi = whi_ref[...]
    out = _dot(h_hi, whi) + _dot(h_lo, whi) + _dot(h_hi, wlo_ref[...])
    low_half = lax.broadcasted_iota(jnp.int32, (1, LANES), 1) < IDX_HEAD_DIM
    npair = IDX_HEADS // 2
    for p in range(npair):
        hi, lo = _split_bf16(out[:, p * LANES:(p + 1) * LANES])
        hi_sw = pltpu.roll(hi, IDX_HEAD_DIM, 1)
        lo_sw = pltpu.roll(lo, IDX_HEAD_DIM, 1)
        base = 2 * p * IDX_CAT
        qcat_ref[:, base:base + LANES] = jnp.where(low_half, hi, lo_sw).astype(BF16)
        qcat_ref[:, base + LANES:base + 2 * LANES] = jnp.where(low_half, hi, 0.0).astype(BF16)
        qcat_ref[:, base + 2 * LANES:base + 3 * LANES] = jnp.where(low_half, hi_sw, lo).astype(BF16)
        qcat_ref[:, base + 3 * LANES:base + 4 * LANES] = jnp.where(low_half, hi_sw, 0.0).astype(BF16)
    kw = out[:, npair * LANES:(npair + 1) * LANES]
    k_hi, k_lo = _split_bf16(kw)
    kcat_ref[:, 0:LANES] = jnp.where(low_half, k_hi, pltpu.roll(k_hi, IDX_HEAD_DIM, 1)).astype(BF16)
    kcat_ref[:, LANES:2 * LANES] = jnp.where(low_half, k_lo, 0.0).astype(BF16)
    wi_ref[...] = pltpu.roll(kw, IDX_HEAD_DIM, 1)


def _proj(x2, g, w_glu, w_qkm, w_vt, w_hi, w_lo):
    tm = DSA_TK
    t, d = x2.shape
    c = w_glu.shape[1] // 2
    nq = w_qkm.shape[1]
    nv = w_vt.shape[0]
    row = lambda i: (i, 0)
    return pl.pallas_call(
        _proj_kernel,
        grid=(t // tm,),
        in_specs=[
            pl.BlockSpec((tm, d), row),
            _resident((1, d)),
            _resident(w_glu.shape),
            _resident(w_qkm.shape),
            _resident(w_vt.shape),
            _resident(w_hi.shape),
            _resident(w_lo.shape),
        ],
        out_specs=[
            pl.BlockSpec((tm, c), row),
            pl.BlockSpec((tm, nq), row),
            pl.BlockSpec((1, nv, tm), lambda i: (i, 0, 0)),
            pl.BlockSpec((tm, IDX_HEADS * IDX_CAT), row),
            pl.BlockSpec((tm, IDX_CAT), row),
            pl.BlockSpec((tm, LANES), row),
        ],
        out_shape=[
            jax.ShapeDtypeStruct((t, c), F32),
            jax.ShapeDtypeStruct((t, nq), BF16),
            jax.ShapeDtypeStruct((t // tm, nv, tm), BF16),
            jax.ShapeDtypeStruct((t, IDX_HEADS * IDX_CAT), BF16),
            jax.ShapeDtypeStruct((t, IDX_CAT), BF16),
            jax.ShapeDtypeStruct((t, LANES), F32),
        ],
        compiler_params=_params(("arbitrary",)),
        name="proj",
    )(x2, g, w_glu, w_qkm, w_vt, w_hi, w_lo)


def _proj_bf16_kernel(x_ref, g_ref, w_ref, o_ref):
    h = _rms(x_ref[...], g_ref[...]).astype(BF16)
    o_ref[...] = _dot(h, w_ref[...]).astype(BF16)


def _proj_bf16(x2, g, w, tm):
    t, d = x2.shape
    n = w.shape[1]
    return pl.pallas_call(
        _proj_bf16_kernel,
        grid=(t // tm,),
        in_specs=[
            pl.BlockSpec((tm, d), lambda i: (i, 0)),
            _resident((1, d)),
            _resident(w.shape),
        ],
        out_specs=pl.BlockSpec((tm, n), lambda i: (i, 0)),
        out_shape=jax.ShapeDtypeStruct((t, n), BF16),
        compiler_params=_params(("arbitrary",)),
        name="proj_mem",
    )(x2, g, w)


CONV_HALO = 32
CONV_ROWS = 64


def _conv_branch_kernel(y_ref, dww_ref, dwb_ref, lng_ref, lnb_ref, pw2_ref, o_ref,
                        ext_ref, cv_ref, *, ts, tiles_per_seq):
    c = y_ref.shape[1]
    first = pl.program_id(0) % tiles_per_seq == 0

    @pl.when(first)
    def _():
        ext_ref[:, 0:CONV_HALO, :] = jnp.zeros((c // LANES, CONV_HALO, LANES), F32)

    @pl.when(jnp.logical_not(first))
    def _():
        ext_ref[:, 0:CONV_HALO, :] = ext_ref[:, ts:ts + CONV_HALO, :]

    off0 = CONV_HALO - (CONV_K - 1)
    for cg in range(c // LANES):
        cs = slice(cg * LANES, (cg + 1) * LANES)
        ext_ref[cg, CONV_HALO:CONV_HALO + ts, :] = y_ref[:, cs]
        for rb in range(ts // CONV_ROWS):
            r0 = rb * CONV_ROWS
            acc = jnp.broadcast_to(dwb_ref[:, cs], (CONV_ROWS, LANES))
            for j in range(CONV_K):
                a0 = r0 + off0 + j
                acc = acc + dww_ref[j:j + 1, cs] * ext_ref[cg, a0:a0 + CONV_ROWS, :]
            cv_ref[r0:r0 + CONV_ROWS, cs] = acc

    y = cv_ref[...]
    mu = jnp.mean(y, axis=-1, keepdims=True)
    yc = y - mu
    var = jnp.mean(yc * yc, axis=-1, keepdims=True)
    z = yc * lax.rsqrt(var + EPS) * lng_ref[...] + lnb_ref[...]
    z = z * jax.nn.sigmoid(z)
    o_ref[...] = _dot(z.astype(BF16), pw2_ref[...])


def _conv_branch(yg, dw_w, dw_b, ln_g, ln_b, pw2, seq, ts):
    t, c = yg.shape
    kern = functools.partial(_conv_branch_kernel, ts=ts, tiles_per_seq=seq // ts)
    return pl.pallas_call(
        kern,
        grid=(t // ts,),
        in_specs=[
            pl.BlockSpec((ts, c), lambda i: (i, 0)),
            _resident(dw_w.shape),
            _resident((1, c)),
            _resident((1, c)),
            _resident((1, c)),
            _resident((c, c)),
        ],
        out_specs=pl.BlockSpec((ts, c), lambda i: (i, 0)),
        out_shape=jax.ShapeDtypeStruct((t, c), F32),
        scratch_shapes=[pltpu.VMEM((c // LANES, ts + CONV_HALO, LANES), F32),
                        pltpu.VMEM((ts, c), F32)],
        compiler_params=_params(("arbitrary",)),
        name="conv_branch",
    )(yg, dw_w, dw_b, ln_g, ln_b, pw2)


DSA_TQ = 256
DSA_TK = 512
DSA_NACC = 4
ATT_NACC = 2
DSA_HG = 2
BISECT_LINEAR = 24
BISECT_UNROLL = 2
BISECT_MAX = 40


def _ordered_key(f):
    b = lax.bitcast_convert_type(f, jnp.int32)
    return b ^ ((b >> 31) & jnp.int32(0x7FFFFFFF))


def _from_ordered_key(k):
    return lax.bitcast_convert_type(k ^ ((k >> 31) & jnp.int32(0x7FFFFFFF)), F32)


def _dsa_kernel(q_ref, k_ref, vt_ref, qcat_ref, kcat_ref, wit_ref, tri_ref, o_ref,
                sc_ref, lg0_ref, lg1_ref, acc_ref, *, topk):
    lg_refs = (lg0_ref, lg1_ref)
    tq, tk = DSA_TQ, DSA_TK
    qt = pl.program_id(1)
    nck = ((qt + 1) * tq + tk - 1) // tk
    idx_scale = (IDX_HEADS ** -0.5) * (IDX_HEAD_DIM ** -0.5)
    kf = float(topk)

    q_pos = qt * tq + lax.broadcasted_iota(jnp.int32, (1, tq), 1)
    key_off = lax.broadcasted_iota(jnp.int32, (tk, 1), 0)

    def chunk_loop(body, init):
        def quad(j, carry):
            for u in range(4):
                carry = body(4 * j + u, carry)
            return carry
        carry = lax.fori_loop(0, lax.shift_right_logical(nck, 2), quad, init)
        c2 = nck & ~3
        carry = lax.cond((nck & 2) == 2, lambda cr: body(c2 + 1, body(c2, cr)), lambda cr: cr, carry)
        return lax.cond((nck & 1) == 1, lambda cr: body(nck - 1, cr), lambda cr: cr, carry)

    def score_chunk(c, carry):
        mn, mx = carry
        r0 = pl.multiple_of(c * tk, tk)
        kc = kcat_ref[pl.ds(r0, tk), :]
        acc = jnp.zeros((tk, tq), F32)
        for h in range(IDX_HEADS):
            d = _dot_nt(kc, qcat_ref[:, h * IDX_CAT:(h + 1) * IDX_CAT])
            acc = acc + jnp.maximum(d, 0.0) * wit_ref[h:h + 1, :]
        s = acc * idx_scale
        causal = (c * tk + key_off) <= q_pos
        lo_s = jnp.where(causal, s, POS_INF)
        hi_s = jnp.where(causal, s, NEG_INF)
        sc_ref[c] = hi_s
        mn = jnp.minimum(mn, jnp.min(lo_s, axis=0, keepdims=True))
        mx = jnp.maximum(mx, jnp.max(hi_s, axis=0, keepdims=True))
        return mn, mx

    row_min, row_max = chunk_loop(
        score_chunk, (jnp.full((1, tq), POS_INF, F32), jnp.full((1, tq), NEG_INF, F32)))

    def count_rows(cand, strict):
        cb = jnp.broadcast_to(cand, (SUBLANES, tq))

        def body(c, parts):
            parts = list(parts)
            for i in range(tk // SUBLANES):
                s = sc_ref[c, i * SUBLANES:(i + 1) * SUBLANES, :]
                hit = (s > cb) if strict else (s >= cb)
                a = i % DSA_NACC
                parts[a] = jnp.where(hit, parts[a] + 1.0, parts[a])
            return tuple(parts)

        parts = lax.fori_loop(0, nck, body,
                              tuple(jnp.zeros((SUBLANES, tq), F32) for _ in range(DSA_NACC)))
        return jnp.sum(functools.reduce(jnp.add, parts), axis=0, keepdims=True)

    n_valid = (q_pos + 1).astype(F32)
    few = n_valid <= kf
    zero = jnp.zeros((1, tq), F32)
    cnt_ge0 = count_rows(zero, strict=False)
    cnt_gt0 = count_rows(zero, strict=True)
    pos = cnt_ge0 >= kf
    lo0 = jnp.where(pos, 0.0, row_min)
    cnt_lo0 = jnp.where(pos, cnt_ge0, n_valid)
    hi0 = jnp.where(pos, _from_ordered_key(_ordered_key(row_max) + 1), 0.0)
    cnt_hi0 = jnp.where(pos, 0.0, cnt_ge0)

    def narrow(mid, cnt, lo, hi, cnt_lo, cnt_hi, done):
        ge = (cnt >= kf) & (done < 0.5)
        lt = (cnt < kf) & (done < 0.5)
        return (jnp.where(ge, mid, lo), jnp.where(lt, mid, hi),
                jnp.where(ge, cnt, cnt_lo), jnp.where(lt, cnt, cnt_hi))

    def any_active(done):
        return jnp.max(1.0 - done).astype(jnp.int32)

    zero_thr = (cnt_gt0 < kf) & pos
    settled = jnp.where(few | zero_thr, 1.0, 0.0)

    def pinned(cnt_lo, cnt_hi):
        return (cnt_lo == kf) | (cnt_hi == kf - 1.0)

    def lin_done(lo, hi, mid, cnt_lo, cnt_hi, done):
        return jnp.where(pinned(cnt_lo, cnt_hi) | (mid == lo) | (mid == hi), 1.0, done)

    def lin_trip(state):
        it, _, lo, hi, mid, cnt_lo, cnt_hi, done = state
        for _ in range(BISECT_UNROLL):
            cnt = count_rows(mid, strict=False)
            lo, hi, cnt_lo, cnt_hi = narrow(mid, cnt, lo, hi, cnt_lo, cnt_hi, done)
            mid = 0.5 * lo + 0.5 * hi
            done = lin_done(lo, hi, mid, cnt_lo, cnt_hi, done)
        return it + BISECT_UNROLL, any_active(done), lo, hi, mid, cnt_lo, cnt_hi, done

    mid0 = 0.5 * lo0 + 0.5 * hi0
    done0 = lin_done(lo0, hi0, mid0, cnt_lo0, cnt_hi0, settled)
    _, _, lo1, hi1, _, cnt_lo1, cnt_hi1, _ = lax.while_loop(
        lambda st: jnp.logical_and(st[1] > 0, st[0] < BISECT_LINEAR), lin_trip,
        (jnp.int32(0), any_active(done0), lo0, hi0, mid0, cnt_lo0, cnt_hi0, done0))

    def lat_done(lo, hi, cnt_lo, cnt_hi, done):
        adjacent = _ordered_key(hi) <= _ordered_key(lo) + 1
        return jnp.where(pinned(cnt_lo, cnt_hi) | adjacent, 1.0, done)

    def lat_trip(state):
        it, _, lo, hi, cnt_lo, cnt_hi, done = state
        klo, khi = _ordered_key(lo), _ordered_key(hi)
        mid = _from_ordered_key((klo >> 1) + (khi >> 1) + (klo & khi & 1))
        cnt = count_rows(mid, strict=False)
        lo, hi, cnt_lo, cnt_hi = narrow(mid, cnt, lo, hi, cnt_lo, cnt_hi, done)
        done = lat_done(lo, hi, cnt_lo, cnt_hi, done)
        return it + 1, any_active(done), lo, hi, cnt_lo, cnt_hi, done

    done1 = lat_done(lo1, hi1, cnt_lo1, cnt_hi1, settled)
    _, _, lo2, hi2, cnt_lo2, cnt_hi2, _ = lax.while_loop(
        lambda st: jnp.logical_and(st[1] > 0, st[0] < BISECT_MAX), lat_trip,
        (jnp.int32(0), any_active(done1), lo1, hi1, cnt_lo1, cnt_hi1, done1))

    def max_below(bound):
        bb = jnp.broadcast_to(bound, (SUBLANES, tq))

        def body(c, parts):
            parts = list(parts)
            for i in range(tk // SUBLANES):
                s = sc_ref[c, i * SUBLANES:(i + 1) * SUBLANES, :]
                a = i % DSA_NACC
                parts[a] = jnp.maximum(parts[a], jnp.where(s < bb, s, NEG_INF))
            return tuple(parts)

        parts = lax.fori_loop(0, nck, body,
                              tuple(jnp.full((SUBLANES, tq), NEG_INF, F32) for _ in range(DSA_NACC)))
        return jnp.max(functools.reduce(jnp.maximum, parts), axis=0, keepdims=True)

    top_below = max_below(hi2)
    by_lo = cnt_lo2 == kf
    by_hi = jnp.logical_not(by_lo) & (cnt_hi2 == kf - 1.0)
    thr = jnp.where(by_hi, top_below, lo2)
    need = jnp.where(by_lo, kf, kf - cnt_hi2)
    thr = jnp.where(few, NEG_INF, jnp.where(zero_thr, 0.0, thr))
    need = jnp.where(few, 0.0, jnp.where(zero_thr, kf - cnt_gt0, need))

    def bias_chunk(c, seen):
        s = sc_ref[c]
        eq = s == thr
        eqf = jnp.where(eq, 1.0, 0.0)
        before = _dot(tri_ref[...], eqf.astype(BF16)) + seen
        keep = (s > thr) | (eq & (before < need))
        sc_ref[c] = jnp.where(keep, 0.0, NEG_INF)
        return seen + jnp.sum(eqf, axis=0, keepdims=True)

    lax.fori_loop(0, nck, bias_chunk, jnp.zeros((1, tq), F32))

    def fold_rows(x, op, parts):
        parts = list(parts)
        for i in range(tk // SUBLANES):
            a = i % ATT_NACC
            parts[a] = op(parts[a], x[i * SUBLANES:(i + 1) * SUBLANES, :])
        return tuple(parts)

    def init_parts(value):
        return tuple(tuple(jnp.full((SUBLANES, tq), value, F32) for _ in range(ATT_NACC))
                     for _ in range(DSA_HG))

    def head_cols(grp, i):
        h = grp * DSA_HG + i
        return slice(h * ATT_HEAD_DIM, (h + 1) * ATT_HEAD_DIM)

    def make_body(grp_l, grp_p, m_rows):
        def body(c, carry):
            mparts, lparts = carry
            r0 = pl.multiple_of(c * tk, tk)
            if grp_l is not None:
                new_m = []
                for i in range(DSA_HG):
                    hs = head_cols(grp_l, i)
                    l = _dot_nt(k_ref[pl.ds(r0, tk), hs], q_ref[:, hs]) + sc_ref[c]
                    lg_refs[grp_l % 2][i, c] = l
                    new_m.append(fold_rows(l, jnp.maximum, mparts[i]))
                mparts = tuple(new_m)
            if grp_p is not None:
                new_l = []
                for i in range(DSA_HG):
                    hs = head_cols(grp_p, i)
                    p = jnp.exp2(lg_refs[grp_p % 2][i, c] - m_rows[i])
                    new_l.append(fold_rows(p, jnp.add, lparts[i]))
                    acc_ref[i] = acc_ref[i] + _dot(vt_ref[c, hs, :], p.astype(BF16))
                lparts = tuple(new_l)
            return mparts, lparts
        return body

    ngrp = ATT_HEADS // DSA_HG
    m_rows = None
    for phase in range(ngrp + 1):
        grp_l = phase if phase < ngrp else None
        grp_p = phase - 1 if phase > 0 else None
        if grp_p is not None:
            acc_ref[...] = jnp.zeros(acc_ref.shape, F32)
        mparts, lparts = chunk_loop(make_body(grp_l, grp_p, m_rows),
                                    (init_parts(NEG_INF), init_parts(0.0)))
        if grp_p is not None:
            for i in range(DSA_HG):
                lsum = jnp.sum(functools.reduce(jnp.add, lparts[i]), axis=0, keepdims=True)
                o_ref[:, head_cols(grp_p, i)] = (acc_ref[i] / lsum).T
        if grp_l is not None:
            m_rows = [jnp.max(functools.reduce(jnp.maximum, mp), axis=0, keepdims=True)
                      for mp in mparts]


def _dsa(qkm, vt, qcat, kcat, wit, tri, batch, seq):
    tq, tk = DSA_TQ, DSA_TK
    d = D_MODEL
    nq = seq // tq
    nc = seq // tk
    topk = min(TOPK_MAX, seq // 4)
    kern = functools.partial(_dsa_kernel, topk=topk)
    qrow = lambda b, i: (b * nq + i, 0)
    return pl.pallas_call(
        kern,
        grid=(batch, nq),
        in_specs=[
            pl.BlockSpec((tq, d), qrow),
            pl.BlockSpec((seq, d), lambda b, i: (b, 1), pipeline_mode=pl.Buffered(1)),
            pl.BlockSpec((nc, d, tk), lambda b, i: (b, 0, 0), pipeline_mode=pl.Buffered(1)),
            pl.BlockSpec((tq, qcat.shape[1]), qrow),
            pl.BlockSpec((seq, kcat.shape[1]), lambda b, i: (b, 0)),
            pl.BlockSpec((IDX_HEADS, tq), lambda b, i: (0, b * nq + i)),
            _resident((tk, tk)),
        ],
        out_specs=pl.BlockSpec((tq, d), qrow),
        out_shape=jax.ShapeDtypeStruct((batch * seq, d), F32),
        scratch_shapes=[
            pltpu.VMEM((nc, tk, tq), F32),
            pltpu.VMEM((DSA_HG, nc, tk, tq), F32),
            pltpu.VMEM((DSA_HG, nc, tk, tq), F32),
            pltpu.VMEM((DSA_HG, ATT_HEAD_DIM, tq), F32),
        ],
        compiler_params=_params(("arbitrary", "arbitrary")),
        name="dsa",
    )(qkm, qkm, vt, qcat, kcat, wit, tri)


def _mem_attn_kernel(qm_ref, mk_ref, mv_ref, o_ref):
    scale = MEM_HEAD_DIM ** -0.5
    for h in range(MEM_HEADS):
        hs = slice(h * MEM_HEAD_DIM, (h + 1) * MEM_HEAD_DIM)
        l = _dot_nt(qm_ref[:, hs], mk_ref[:, hs]) * scale
        p = jnp.exp(l - jnp.max(l, axis=1, keepdims=True))
        acc = _dot(p.astype(BF16), mv_ref[:, hs])
        o_ref[:, hs] = acc / jnp.sum(p, axis=1, keepdims=True)


def _mem_attn(qkm, mkv, batch, seq, n_mem, tm):
    d = D_MODEL
    nt = seq // tm
    return pl.pallas_call(
        _mem_attn_kernel,
        grid=(batch, nt),
        in_specs=[
            pl.BlockSpec((tm, d), lambda b, i: (b * nt + i, 2)),
            pl.BlockSpec((n_mem, d), lambda b, i: (b, 0)),
            pl.BlockSpec((n_mem, d), lambda b, i: (b, 1)),
        ],
        out_specs=pl.BlockSpec((tm, d), lambda b, i: (b * nt + i, 0)),
        out_shape=jax.ShapeDtypeStruct((batch * seq, d), F32),
        compiler_params=_params(("arbitrary", "arbitrary")),
        name="mem_attn",
    )(qkm, mkv, mkv)


def _merge_kernel(x_ref, yc_ref, ya_ref, ym_ref, g1_ref, wg_ref, bg_ref, wo_ref, gp_ref, o_ref):
    d = x_ref.shape[1]
    x = x_ref[...]
    h = _rms(x, g1_ref[...]).astype(BF16)
    merged = jnp.zeros(x.shape, F32)
    for br, y_ref in enumerate((yc_ref, ya_ref, ym_ref)):
        cs = slice(br * d, (br + 1) * d)
        gate = jax.nn.sigmoid(_dot(h, wg_ref[:, cs]) + bg_ref[:, cs])
        merged = merged + gate * y_ref[...]
    out = _dot(merged.astype(BF16), wo_ref[...])
    o_ref[...] = x + _rms(out, gp_ref[...])


def _merge(x2, yc, ya, ym, g1, w_gate, b_gate, w_out, g_post, tm):
    t, d = x2.shape
    row = lambda i: (i, 0)
    return pl.pallas_call(
        _merge_kernel,
        grid=(t // tm,),
        in_specs=[
            pl.BlockSpec((tm, d), row), pl.BlockSpec((tm, d), row),
            pl.BlockSpec((tm, d), row), pl.BlockSpec((tm, d), row),
            _resident((1, d)),
            _resident((d, 3 * d)),
            _resident((1, 3 * d)),
            _resident((d, d)),
            _resident((1, d)),
        ],
        out_specs=pl.BlockSpec((tm, d), row),
        out_shape=jax.ShapeDtypeStruct((t, d), F32),
        compiler_params=_params(("arbitrary",)),
        name="merge_out",
    )(x2, yc, ya, ym, g1, w_gate, b_gate, w_out, g_post)


FFN_FC = 2816


def _ffn_kernel(x_ref, g2_ref, wu_ref, cw_ref, cb_ref, wd_ref, gp_ref, o_ref,
                ext_ref, tail_ref, *, tm, tiles_per_seq):
    first = pl.program_id(0) % tiles_per_seq == 0
    halo = SUBLANES
    f = wd_ref.shape[0]
    fc = FFN_FC
    x = x_ref[...]
    h = _rms(x, g2_ref[...]).astype(BF16)

    @pl.when(pl.program_id(0) == 0)
    def _():
        tail_ref[...] = jnp.zeros(tail_ref.shape, F32)

    def conv_half(col0, slab0):
        up = _dot(h, wu_ref[:, col0:col0 + fc])
        outs = []
        for s in range(fc // LANES):
            piece = up[:, s * LANES:(s + 1) * LANES]
            cs = slice(col0 + s * LANES, col0 + (s + 1) * LANES)
            slab = slab0 + s
            ext_ref[slab, 0:halo, :] = jnp.where(first, 0.0, tail_ref[slab])
            ext_ref[slab, halo:halo + tm, :] = piece
            tail_ref[slab] = piece[tm - halo:tm, :]
            u = cb_ref[:, cs] + cw_ref[2:3, cs] * piece
            u = u + cw_ref[1:2, cs] * ext_ref[slab, halo - 1:halo - 1 + tm, :]
            u = u + cw_ref[0:1, cs] * ext_ref[slab, halo - 2:halo - 2 + tm, :]
            outs.append(u)
        return jnp.concatenate(outs, axis=1)

    acc = None
    nslab = fc // LANES
    for j in range(f // fc):
        ug = conv_half(j * fc, 2 * j * nslab)
        uv = conv_half(f + j * fc, (2 * j + 1) * nslab)
        act = (ug * jax.nn.sigmoid(ug)) * uv
        part = _dot(act.astype(BF16), wd_ref[j * fc:(j + 1) * fc, :])
        acc = part if acc is None else acc + part
    o_ref[...] = x + _rms(acc, gp_ref[...])


def _ffn(x1, g2, w_up, cw, cb, w_down, g_post, seq, tm):
    t, d = x1.shape
    f = w_down.shape[0]
    fc = FFN_FC
    kern = functools.partial(_ffn_kernel, tm=tm, tiles_per_seq=seq // tm)
    return pl.pallas_call(
        kern,
        grid=(t // tm,),
        in_specs=[
            pl.BlockSpec((tm, d), lambda i: (i, 0)),
            _resident((1, d)),
            _resident(w_up.shape),
            _resident(cw.shape),
            _resident(cb.shape),
            _resident(w_down.shape),
            _resident((1, d)),
        ],
        out_specs=pl.BlockSpec((tm, d), lambda i: (i, 0)),
        out_shape=jax.ShapeDtypeStruct((t, d), F32),
        scratch_shapes=[
            pltpu.VMEM((2 * f // LANES, tm + SUBLANES, LANES), F32),
            pltpu.VMEM((2 * f // LANES, SUBLANES, LANES), F32),
        ],
        compiler_params=_params(("arbitrary",)),
        name="ffn",
    )(x1, g2, w_up, cw, cb, w_down, g_post)


def _pad_rows(a, rows):
    return jnp.pad(a, ((0, rows - a.shape[0]), (0, 0)))


def _pad_cols(a, cols):
    return jnp.pad(a, ((0, 0), (0, cols - a.shape[1])))


def _layer(x2, mem2, batch, seq, n_mem, p):
    d = D_MODEL
    row = lambda v: v.reshape(1, -1)
    w_in = p["w_in"]
    c2 = 2 * d
    o_q, o_qi = c2, c2 + 3 * d
    o_wi = o_qi + IDX_HEADS * IDX_HEAD_DIM
    o_ki = o_wi + IDX_HEADS
    o_qm = o_ki + IDX_HEAD_DIM

    w_glu = w_in[:, :c2].astype(BF16)
    w_qkm = jnp.concatenate([w_in[:, o_q:o_q + 2 * d], w_in[:, o_qm:]], axis=1).astype(BF16)
    w_vt = w_in[:, o_q + 2 * d:o_qi].T.astype(BF16)
    w_idx = jnp.concatenate(
        [w_in[:, o_qi:o_wi],
         _pad_cols(jnp.concatenate([w_in[:, o_ki:o_qm], w_in[:, o_wi:o_ki]], axis=1), LANES)], axis=1)
    w_idx_hi = w_idx.astype(BF16)
    w_idx_lo = (w_idx - w_idx_hi.astype(F32)).astype(BF16)

    g1 = row(p["norm1_pre_g"])
    yg, qkm, vt, qcat, kcat, wi = _proj(x2, g1, w_glu, w_qkm, w_vt, w_idx_hi, w_idx_lo)

    dw_w = _pad_rows(p["conv_dw_w"], 32)
    y_conv = _conv_branch(yg, dw_w, row(p["conv_dw_b"]), row(p["conv_ln_g"]),
                          row(p["conv_ln_b"]), p["conv_pw2"].astype(BF16), seq, ts=512)

    tk = DSA_TK
    tri = (lax.broadcasted_iota(jnp.int32, (tk, tk), 1)
           < lax.broadcasted_iota(jnp.int32, (tk, tk), 0)).astype(BF16)
    wit = wi[:, :IDX_HEADS].T
    y_att = _dsa(qkm, vt, qcat, kcat, wit, tri, batch, seq)

    mkv = _proj_bf16(mem2, row(p["mem_norm_g"]), p["w_mem_kv"].astype(BF16), tm=n_mem)
    y_mem = _mem_attn(qkm, mkv, batch, seq, n_mem, tm=512)

    x1 = _merge(x2, y_conv, y_att, y_mem, g1, p["w_gate"].astype(BF16), row(p["b_gate"]),
                p["w_out"].astype(BF16), row(p["norm1_post_g"]), tm=256)

    x2o = _ffn(x1, row(p["norm2_pre_g"]), p["w_up"].astype(BF16),
               _pad_rows(p["ffn_dw_w"], SUBLANES), row(p["ffn_dw_b"]),
               p["w_down"].astype(BF16), row(p["norm2_post_g"]), seq, tm=512)
    return x2o


def kernel(x, mem, norm1_pre_g, w_in, conv_dw_w, conv_dw_b, conv_ln_g, conv_ln_b, conv_pw2,
           mem_norm_g, w_mem_kv, w_gate, b_gate, w_out, norm1_post_g, norm2_pre_g, w_up,
           ffn_dw_w, ffn_dw_b, w_down, norm2_post_g):
    batch, seq, d = x.shape
    n_mem = mem.shape[1]
    assert d == D_MODEL and seq % DSA_TK == 0 and n_mem % SUBLANES == 0
    names = ("norm1_pre_g", "w_in", "conv_dw_w", "conv_dw_b", "conv_ln_g", "conv_ln_b",
             "conv_pw2", "mem_norm_g", "w_mem_kv", "w_gate", "b_gate", "w_out", "norm1_post_g",
             "norm2_pre_g", "w_up", "ffn_dw_w", "ffn_dw_b", "w_down", "norm2_post_g")
    vals = (norm1_pre_g, w_in, conv_dw_w, conv_dw_b, conv_ln_g, conv_ln_b, conv_pw2,
            mem_norm_g, w_mem_kv, w_gate, b_gate, w_out, norm1_post_g, norm2_pre_g, w_up,
            ffn_dw_w, ffn_dw_b, w_down, norm2_post_g)
    x2 = x.reshape(batch * seq, d)
    mem2 = mem.reshape(batch * n_mem, d)
    for l in range(norm1_pre_g.shape[0]):
        x2 = _layer(x2, mem2, batch, seq, n_mem, {n: v[l] for n, v in zip(names, vals)})
    return x2.reshape(batch, seq, d)
```

```python
import functools

import jax
import jax.numpy as jnp
from jax import lax
from jax.experimental import pallas as pl
from jax.experimental.pallas import tpu as pltpu

EPS = 1e-6
D_MODEL = 1024
CONV_K = 31
ATT_HEADS = 8
ATT_HEAD_DIM = 128
IDX_HEADS = 8
IDX_HEAD_DIM = 64
TOPK_MAX = 256
MEM_HEADS = 4
MEM_HEAD_DIM = 256

LANES = 128
SUBLANES = 8
VMEM_LIMIT_BYTES = 56 * 1024 * 1024

F32 = jnp.float32
BF16 = jnp.bfloat16
NEG_INF = float("-inf")
POS_INF = float("inf")


def _params(semantics):
    return pltpu.CompilerParams(dimension_semantics=semantics,
                                vmem_limit_bytes=VMEM_LIMIT_BYTES)


def _resident(shape):
    return pl.BlockSpec(shape, lambda *_: (0,) * len(shape), pipeline_mode=pl.Buffered(1))


def _dot(a, b):
    return jnp.dot(a, b, preferred_element_type=F32)


def _dot_nt(a, b):
    return lax.dot_general(a, b, (((1,), (1,)), ((), ())), preferred_element_type=F32)


def _rms(x, g):
    return x * lax.rsqrt(jnp.mean(x * x, axis=-1, keepdims=True) + EPS) * g


def _split_bf16(v):
    hi = v.astype(BF16).astype(F32)
    lo = (v - hi).astype(BF16).astype(F32)
    return hi, lo


PROJ_TN = 512
Q_PRESCALE = ATT_HEAD_DIM ** -0.5 * 1.4426950408889634
IDX_CAT = 2 * LANES


def _proj_kernel(x_ref, g_ref, wglu_ref, wqkm_ref, wvt_ref, whi_ref, wlo_ref,
                 yg_ref, qkm_ref, vt_ref, qcat_ref, kcat_ref, wi_ref):
    h = _rms(x_ref[...], g_ref[...])
    h_hi = h.astype(BF16)
    h_lo = (h - h_hi.astype(F32)).astype(BF16)

    c = yg_ref.shape[1]
    for j in range(c // PROJ_TN):
        a = _dot(h_hi, wglu_ref[:, j * PROJ_TN:(j + 1) * PROJ_TN])
        gate = _dot(h_hi, wglu_ref[:, c + j * PROJ_TN:c + (j + 1) * PROJ_TN])
        yg_ref[:, j * PROJ_TN:(j + 1) * PROJ_TN] = a * jax.nn.sigmoid(gate)

    for j in range(qkm_ref.shape[1] // PROJ_TN):
        cs = slice(j * PROJ_TN, (j + 1) * PROJ_TN)
        r = _dot(h_hi, wqkm_ref[:, cs])
        if (j + 1) * PROJ_TN <= ATT_HEADS * ATT_HEAD_DIM:
            r = r * Q_PRESCALE
        qkm_ref[:, cs] = r.astype(BF16)

    for j in range(vt_ref.shape[1] // PROJ_TN):
        cs = slice(j * PROJ_TN, (j + 1) * PROJ_TN)
        vt_ref[0, cs, :] = _dot_nt(wvt_ref[cs, :], h_hi).astype(BF16)

    whi = whi_ref[...]
    out = _dot(h_hi, whi) + _dot(h_lo, whi) + _dot(h_hi, wlo_ref[...])
    low_half = lax.broadcasted_iota(jnp.int32, (1, LANES), 1) < IDX_HEAD_DIM
    npair = IDX_HEADS // 2
    for p in range(npair):
        hi, lo = _split_bf16(out[:, p * LANES:(p + 1) * LANES])
        hi_sw = pltpu.roll(hi, IDX_HEAD_DIM, 1)
        lo_sw = pltpu.roll(lo, IDX_HEAD_DIM, 1)
        base = 2 * p * IDX_CAT
        qcat_ref[:, base:base + LANES] = jnp.where(low_half, hi, lo_sw).astype(BF16)
        qcat_ref[:, base + LANES:base + 2 * LANES] = jnp.where(low_half, hi, 0.0).astype(BF16)
        qcat_ref[:, base + 2 * LANES:base + 3 * LANES] = jnp.where(low_half, hi_sw, lo).astype(BF16)
        qcat_ref[:, base + 3 * LANES:base + 4 * LANES] = jnp.where(low_half, hi_sw, 0.0).astype(BF16)
    kw = out[:, npair * LANES:(npair + 1) * LANES]
    k_hi, k_lo = _split_bf16(kw)
    kcat_ref[:, 0:LANES] = jnp.where(low_half, k_hi, pltpu.roll(k_hi, IDX_HEAD_DIM, 1)).astype(BF16)
    kcat_ref[:, LANES:2 * LANES] = jnp.where(low_half, k_lo, 0.0).astype(BF16)
    wi_ref[...] = pltpu.roll(kw, IDX_HEAD_DIM, 1)


def _proj(x2, g, w_glu, w_qkm, w_vt, w_hi, w_lo):
    tm = DSA_TK
    t, d = x2.shape
    c = w_glu.shape[1] // 2
    nq = w_qkm.shape[1]
    nv = w_vt.shape[0]
    row = lambda i: (i, 0)
    return pl.pallas_call(
        _proj_kernel,
        grid=(t // tm,),
        in_specs=[
            pl.BlockSpec((tm, d), row),
            _resident((1, d)),
            _resident(w_glu.shape),
            _resident(w_qkm.shape),
            _resident(w_vt.shape),
            _resident(w_hi.shape),
            _resident(w_lo.shape),
        ],
        out_specs=[
            pl.BlockSpec((tm, c), row),
            pl.BlockSpec((tm, nq), row),
            pl.BlockSpec((1, nv, tm), lambda i: (i, 0, 0)),
            pl.BlockSpec((tm, IDX_HEADS * IDX_CAT), row),
            pl.BlockSpec((tm, IDX_CAT), row),
            pl.BlockSpec((tm, LANES), row),
        ],
        out_shape=[
            jax.ShapeDtypeStruct((t, c), F32),
            jax.ShapeDtypeStruct((t, nq), BF16),
            jax.ShapeDtypeStruct((t // tm, nv, tm), BF16),
            jax.ShapeDtypeStruct((t, IDX_HEADS * IDX_CAT), BF16),
            jax.ShapeDtypeStruct((t, IDX_CAT), BF16),
            jax.ShapeDtypeStruct((t, LANES), F32),
        ],
        compiler_params=_params(("arbitrary",)),
        name="proj",
    )(x2, g, w_glu, w_qkm, w_vt, w_hi, w_lo)


def _proj_bf16_kernel(x_ref, g_ref, w_ref, o_ref):
    h = _rms(x_ref[...], g_ref[...]).astype(BF16)
    o_ref[...] = _dot(h, w_ref[...]).astype(BF16)


def _proj_bf16(x2, g, w, tm):
    t, d = x2.shape
    n = w.shape[1]
    return pl.pallas_call(
        _proj_bf16_kernel,
        grid=(t // tm,),
        in_specs=[
            pl.BlockSpec((tm, d), lambda i: (i, 0)),
            _resident((1, d)),
            _resident(w.shape),
        ],
        out_specs=pl.BlockSpec((tm, n), lambda i: (i, 0)),
        out_shape=jax.ShapeDtypeStruct((t, n), BF16),
        compiler_params=_params(("arbitrary",)),
        name="proj_mem",
    )(x2, g, w)


CONV_HALO = 32
CONV_ROWS = 64


def _conv_branch_kernel(y_ref, dww_ref, dwb_ref, lng_ref, lnb_ref, pw2_ref, o_ref,
                        ext_ref, cv_ref, *, ts, tiles_per_seq):
    c = y_ref.shape[1]
    first = pl.program_id(0) % tiles_per_seq == 0

    @pl.when(first)
    def _():
        ext_ref[:, 0:CONV_HALO, :] = jnp.zeros((c // LANES, CONV_HALO, LANES), F32)

    @pl.when(jnp.logical_not(first))
    def _():
        ext_ref[:, 0:CONV_HALO, :] = ext_ref[:, ts:ts + CONV_HALO, :]

    off0 = CONV_HALO - (CONV_K - 1)
    for cg in range(c // LANES):
        cs = slice(cg * LANES, (cg + 1) * LANES)
        ext_ref[cg, CONV_HALO:CONV_HALO + ts, :] = y_ref[:, cs]
        for rb in range(ts // CONV_ROWS):
            r0 = rb * CONV_ROWS
            acc = jnp.broadcast_to(dwb_ref[:, cs], (CONV_ROWS, LANES))
            for j in range(CONV_K):
                a0 = r0 + off0 + j
                acc = acc + dww_ref[j:j + 1, cs] * ext_ref[cg, a0:a0 + CONV_ROWS, :]
            cv_ref[r0:r0 + CONV_ROWS, cs] = acc

    y = cv_ref[...]
    mu = jnp.mean(y, axis=-1, keepdims=True)
    yc = y - mu
    var = jnp.mean(yc * yc, axis=-1, keepdims=True)
    z = yc * lax.rsqrt(var + EPS) * lng_ref[...] + lnb_ref[...]
    z = z * jax.nn.sigmoid(z)
    o_ref[...] = _dot(z.astype(BF16), pw2_ref[...])


def _conv_branch(yg, dw_w, dw_b, ln_g, ln_b, pw2, seq, ts):
    t, c = yg.shape
    kern = functools.partial(_conv_branch_kernel, ts=ts, tiles_per_seq=seq // ts)
    return pl.pallas_call(
        kern,
        grid=(t // ts,),
        in_specs=[
            pl.BlockSpec((ts, c), lambda i: (i, 0)),
            _resident(dw_w.shape),
            _resident((1, c)),
            _resident((1, c)),
            _resident((1, c)),
            _resident((c, c)),
        ],
        out_specs=pl.BlockSpec((ts, c), lambda i: (i, 0)),
        out_shape=jax.ShapeDtypeStruct((t, c), F32),
        scratch_shapes=[pltpu.VMEM((c // LANES, ts + CONV_HALO, LANES), F32),
                        pltpu.VMEM((ts, c), F32)],
        compiler_params=_params(("arbitrary",)),
        name="conv_branch",
    )(yg, dw_w, dw_b, ln_g, ln_b, pw2)


DSA_TQ = 256
DSA_TK = 512
DSA_NACC = 4
ATT_NACC = 2
DSA_HG = 2
BISECT_LINEAR = 24
BISECT_UNROLL = 2
BISECT_MAX = 40


def _ordered_key(f):
    b = lax.bitcast_convert_type(f, jnp.int32)
    return b ^ ((b >> 31) & jnp.int32(0x7FFFFFFF))


def _from_ordered_key(k):
    return lax.bitcast_convert_type(k ^ ((k >> 31) & jnp.int32(0x7FFFFFFF)), F32)


def _dsa_kernel(q_ref, k_ref, vt_ref, qcat_ref, kcat_ref, wit_ref, tri_ref, o_ref,
                sc_ref, lg0_ref, lg1_ref, acc_ref, *, topk):
    lg_refs = (lg0_ref, lg1_ref)
    tq, tk = DSA_TQ, DSA_TK
    qt = pl.program_id(1)
    nck = ((qt + 1) * tq + tk - 1) // tk
    idx_scale = (IDX_HEADS ** -0.5) * (IDX_HEAD_DIM ** -0.5)
    kf = float(topk)

    q_pos = qt * tq + lax.broadcasted_iota(jnp.int32, (1, tq), 1)
    key_off = lax.broadcasted_iota(jnp.int32, (tk, 1), 0)

    def chunk_loop(body, init):
        def quad(j, carry):
            for u in range(4):
                carry = body(4 * j + u, carry)
            return carry
        carry = lax.fori_loop(0, lax.shift_right_logical(nck, 2), quad, init)
        c2 = nck & ~3
        carry = lax.cond((nck & 2) == 2, lambda cr: body(c2 + 1, body(c2, cr)), lambda cr: cr, carry)
        return lax.cond((nck & 1) == 1, lambda cr: body(nck - 1, cr), lambda cr: cr, carry)

    def score_chunk(c, carry):
        mn, mx = carry
        r0 = pl.multiple_of(c * tk, tk)
        kc = kcat_ref[pl.ds(r0, tk), :]
        acc = jnp.zeros((tk, tq), F32)
        for h in range(IDX_HEADS):
            d = _dot_nt(kc, qcat_ref[:, h * IDX_CAT:(h + 1) * IDX_CAT])
            acc = acc + jnp.maximum(d, 0.0) * wit_ref[h:h + 1, :]
        s = acc * idx_scale
        causal = (c * tk + key_off) <= q_pos
        lo_s = jnp.where(causal, s, POS_INF)
        hi_s = jnp.where(causal, s, NEG_INF)
        sc_ref[c] = hi_s
        mn = jnp.minimum(mn, jnp.min(lo_s, axis=0, keepdims=True))
        mx = jnp.maximum(mx, jnp.max(hi_s, axis=0, keepdims=True))
        return mn, mx

    row_min, row_max = chunk_loop(
        score_chunk, (jnp.full((1, tq), POS_INF, F32), jnp.full((1, tq), NEG_INF, F32)))

    def count_rows(cand, strict):
        cb = jnp.broadcast_to(cand, (SUBLANES, tq))

        def body(c, parts):
            parts = list(parts)
            for i in range(tk // SUBLANES):
                s = sc_ref[c, i * SUBLANES:(i + 1) * SUBLANES, :]
                hit = (s > cb) if strict else (s >= cb)
                a = i % DSA_NACC
                parts[a] = jnp.where(hit, parts[a] + 1.0, parts[a])
            return tuple(parts)

        parts = lax.fori_loop(0, nck, body,
                              tuple(jnp.zeros((SUBLANES, tq), F32) for _ in range(DSA_NACC)))
        return jnp.sum(functools.reduce(jnp.add, parts), axis=0, keepdims=True)

    n_valid = (q_pos + 1).astype(F32)
    few = n_valid <= kf
    zero = jnp.zeros((1, tq), F32)
    cnt_ge0 = count_rows(zero, strict=False)
    cnt_gt0 = count_rows(zero, strict=True)
    pos = cnt_ge0 >= kf
    lo0 = jnp.where(pos, 0.0, row_min)
    cnt_lo0 = jnp.where(pos, cnt_ge0, n_valid)
    hi0 = jnp.where(pos, _from_ordered_key(_ordered_key(row_max) + 1), 0.0)
    cnt_hi0 = jnp.where(pos, 0.0, cnt_ge0)

    def narrow(mid, cnt, lo, hi, cnt_lo, cnt_hi, done):
        ge = (cnt >= kf) & (done < 0.5)
        lt = (cnt < kf) & (done < 0.5)
        return (jnp.where(ge, mid, lo), jnp.where(lt, mid, hi),
                jnp.where(ge, cnt, cnt_lo), jnp.where(lt, cnt, cnt_hi))

    def any_active(done):
        return jnp.max(1.0 - done).astype(jnp.int32)

    zero_thr = (cnt_gt0 < kf) & pos
    settled = jnp.where(few | zero_thr, 1.0, 0.0)

    def pinned(cnt_lo, cnt_hi):
        return (cnt_lo == kf) | (cnt_hi == kf - 1.0)

    def lin_done(lo, hi, mid, cnt_lo, cnt_hi, done):
        return jnp.where(pinned(cnt_lo, cnt_hi) | (mid == lo) | (mid == hi), 1.0, done)

    def lin_trip(state):
        it, _, lo, hi, mid, cnt_lo, cnt_hi, done = state
        for _ in range(BISECT_UNROLL):
            cnt = count_rows(mid, strict=False)
            lo, hi, cnt_lo, cnt_hi = narrow(mid, cnt, lo, hi, cnt_lo, cnt_hi, done)
            mid = 0.5 * lo + 0.5 * hi
            done = lin_done(lo, hi, mid, cnt_lo, cnt_hi, done)
        return it + BISECT_UNROLL, any_active(done), lo, hi, mid, cnt_lo, cnt_hi, done

    mid0 = 0.5 * lo0 + 0.5 * hi0
    done0 = lin_done(lo0, hi0, mid0, cnt_lo0, cnt_hi0, settled)
    _, _, lo1, hi1, _, cnt_lo1, cnt_hi1, _ = lax.while_loop(
        lambda st: jnp.logical_and(st[1] > 0, st[0] < BISECT_LINEAR), lin_trip,
        (jnp.int32(0), any_active(done0), lo0, hi0, mid0, cnt_lo0, cnt_hi0, done0))

    def lat_done(lo, hi, cnt_lo, cnt_hi, done):
        adjacent = _ordered_key(hi) <= _ordered_key(lo) + 1
        return jnp.where(pinned(cnt_lo, cnt_hi) | adjacent, 1.0, done)

    def lat_trip(state):
        it, _, lo, hi, cnt_lo, cnt_hi, done = state
        klo, khi = _ordered_key(lo), _ordered_key(hi)
        mid = _from_ordered_key((klo >> 1) + (khi >> 1) + (klo & khi & 1))
        cnt = count_rows(mid, strict=False)
        lo, hi, cnt_lo, cnt_hi = narrow(mid, cnt, lo, hi, cnt_lo, cnt_hi, done)
        done = lat_done(lo, hi, cnt_lo, cnt_hi, done)
        return it + 1, any_active(done), lo, hi, cnt_lo, cnt_hi, done

    done1 = lat_done(lo1, hi1, cnt_lo1, cnt_hi1, settled)
    _, _, lo2, hi2, cnt_lo2, cnt_hi2, _ = lax.while_loop(
        lambda st: jnp.logical_and(st[1] > 0, st[0] < BISECT_MAX), lat_trip,
        (jnp.int32(0), any_active(done1), lo1, hi1, cnt_lo1, cnt_hi1, done1))

    def max_below(bound):
        bb = jnp.broadcast_to(bound, (SUBLANES, tq))

        def body(c, parts):
            parts = list(parts)
            for i in range(tk // SUBLANES):
                s = sc_ref[c, i * SUBLANES:(i + 1) * SUBLANES, :]
                a = i % DSA_NACC
                parts[a] = jnp.maximum(parts[a], jnp.where(s < bb, s, NEG_INF))
            return tuple(parts)

        parts = lax.fori_loop(0, nck, body,
                              tuple(jnp.full((SUBLANES, tq), NEG_INF, F32) for _ in range(DSA_NACC)))
        return jnp.max(functools.reduce(jnp.maximum, parts), axis=0, keepdims=True)

    top_below = max_below(hi2)
    by_lo = cnt_lo2 == kf
    by_hi = jnp.logical_not(by_lo) & (cnt_hi2 == kf - 1.0)
    thr = jnp.where(by_hi, top_below, lo2)
    need = jnp.where(by_lo, kf, kf - cnt_hi2)
    thr = jnp.where(few, NEG_INF, jnp.where(zero_thr, 0.0, thr))
    need = jnp.where(few, 0.0, jnp.where(zero_thr, kf - cnt_gt0, need))

    def bias_chunk(c, seen):
        s = sc_ref[c]
        eq = s == thr
        eqf = jnp.where(eq, 1.0, 0.0)
        before = _dot(tri_ref[...], eqf.astype(BF16)) + seen
        keep = (s > thr) | (eq & (before < need))
        sc_ref[c] = jnp.where(keep, 0.0, NEG_INF)
        return seen + jnp.sum(eqf, axis=0, keepdims=True)

    chunk_loop(bias_chunk, jnp.zeros((1, tq), F32))

    def fold_rows(x, op, parts):
        parts = list(parts)
        for i in range(tk // SUBLANES):
            a = i % ATT_NACC
            parts[a] = op(parts[a], x[i * SUBLANES:(i + 1) * SUBLANES, :])
        return tuple(parts)

    def init_parts(value):
        return tuple(tuple(jnp.full((SUBLANES, tq), value, F32) for _ in range(ATT_NACC))
                     for _ in range(DSA_HG))

    def head_cols(grp, i):
        h = grp * DSA_HG + i
        return slice(h * ATT_HEAD_DIM, (h + 1) * ATT_HEAD_DIM)

    def make_body(grp_l, grp_p, m_rows):
        def body(c, carry):
            mparts, lparts = carry
            r0 = pl.multiple_of(c * tk, tk)
            if grp_l is not None:
                new_m = []
                for i in range(DSA_HG):
                    hs = head_cols(grp_l, i)
                    l = _dot_nt(k_ref[pl.ds(r0, tk), hs], q_ref[:, hs]) + sc_ref[c]
                    lg_refs[grp_l % 2][i, c] = l
                    new_m.append(fold_rows(l, jnp.maximum, mparts[i]))
                mparts = tuple(new_m)
            if grp_p is not None:
                new_l = []
                for i in range(DSA_HG):
                    hs = head_cols(grp_p, i)
                    p = jnp.exp2(lg_refs[grp_p % 2][i, c] - m_rows[i])
                    new_l.append(fold_rows(p, jnp.add, lparts[i]))
                    acc_ref[i] = acc_ref[i] + _dot(vt_ref[c, hs, :], p.astype(BF16))
                lparts = tuple(new_l)
            return mparts, lparts
        return body

    ngrp = ATT_HEADS // DSA_HG
    m_rows = None
    for phase in range(ngrp + 1):
        grp_l = phase if phase < ngrp else None
        grp_p = phase - 1 if phase > 0 else None
        if grp_p is not None:
            acc_ref[...] = jnp.zeros(acc_ref.shape, F32)
        mparts, lparts = chunk_loop(make_body(grp_l, grp_p, m_rows),
                                    (init_parts(NEG_INF), init_parts(0.0)))
        if grp_p is not None:
            for i in range(DSA_HG):
                lsum = jnp.sum(functools.reduce(jnp.add, lparts[i]), axis=0, keepdims=True)
                o_ref[:, head_cols(grp_p, i)] = (acc_ref[i] / lsum).T
        if grp_l is not None:
            m_rows = [jnp.max(functools.reduce(jnp.maximum, mp), axis=0, keepdims=True)
                      for mp in mparts]


def _dsa(qkm, vt, qcat, kcat, wit, tri, batch, seq):
    tq, tk = DSA_TQ, DSA_TK
    d = D_MODEL
    nq = seq // tq
    nc = seq // tk
    topk = min(TOPK_MAX, seq // 4)
    kern = functools.partial(_dsa_kernel, topk=topk)
    qrow = lambda b, i: (b * nq + i, 0)
    return pl.pallas_call(
        kern,
        grid=(batch, nq),
        in_specs=[
            pl.BlockSpec((tq, d), qrow),
            pl.BlockSpec((seq, d), lambda b, i: (b, 1), pipeline_mode=pl.Buffered(1)),
            pl.BlockSpec((nc, d, tk), lambda b, i: (b, 0, 0), pipeline_mode=pl.Buffered(1)),
            pl.BlockSpec((tq, qcat.shape[1]), qrow),
            pl.BlockSpec((seq, kcat.shape[1]), lambda b, i: (b, 0)),
            pl.BlockSpec((IDX_HEADS, tq), lambda b, i: (0, b * nq + i)),
            _resident((tk, tk)),
        ],
        out_specs=pl.BlockSpec((tq, d), qrow),
        out_shape=jax.ShapeDtypeStruct((batch * seq, d), F32),
        scratch_shapes=[
            pltpu.VMEM((nc, tk, tq), F32),
            pltpu.VMEM((DSA_HG, nc, tk, tq), F32),
            pltpu.VMEM((DSA_HG, nc, tk, tq), F32),
            pltpu.VMEM((DSA_HG, ATT_HEAD_DIM, tq), F32),
        ],
        compiler_params=_params(("arbitrary", "arbitrary")),
        name="dsa",
    )(qkm, qkm, vt, qcat, kcat, wit, tri)


def _mem_attention(qm_ref, mk_ref, mv_ref):
    scale = MEM_HEAD_DIM ** -0.5
    outs = []
    for h in range(MEM_HEADS):
        hs = slice(h * MEM_HEAD_DIM, (h + 1) * MEM_HEAD_DIM)
        l = _dot_nt(qm_ref[:, hs], mk_ref[:, hs]) * scale
        p = jnp.exp(l - jnp.max(l, axis=1, keepdims=True))
        acc = _dot(p.astype(BF16), mv_ref[:, hs])
        outs.append(acc / jnp.sum(p, axis=1, keepdims=True))
    return jnp.concatenate(outs, axis=1)


def _merge_kernel(x_ref, yc_ref, ya_ref, qm_ref, mk_ref, mv_ref, g1_ref, wg_ref, bg_ref,
                  wo_ref, gp_ref, o_ref):
    d = x_ref.shape[1]
    x = x_ref[...]
    h = _rms(x, g1_ref[...]).astype(BF16)
    branches = (yc_ref[...], ya_ref[...], _mem_attention(qm_ref, mk_ref, mv_ref))
    merged = jnp.zeros(x.shape, F32)
    for br, y in enumerate(branches):
        cs = slice(br * d, (br + 1) * d)
        gate = jax.nn.sigmoid(_dot(h, wg_ref[:, cs]) + bg_ref[:, cs])
        merged = merged + gate * y
    out = _dot(merged.astype(BF16), wo_ref[...])
    o_ref[...] = x + _rms(out, gp_ref[...])


def _merge(x2, yc, ya, qkm, mkv, g1, w_gate, b_gate, w_out, g_post, seq, n_mem, tm):
    t, d = x2.shape
    row = lambda i: (i, 0)
    tiles_per_seq = seq // tm
    return pl.pallas_call(
        _merge_kernel,
        grid=(t // tm,),
        in_specs=[
            pl.BlockSpec((tm, d), row), pl.BlockSpec((tm, d), row),
            pl.BlockSpec((tm, d), row),
            pl.BlockSpec((tm, d), lambda i: (i, 2)),
            pl.BlockSpec((n_mem, d), lambda i: (i // tiles_per_seq, 0)),
            pl.BlockSpec((n_mem, d), lambda i: (i // tiles_per_seq, 1)),
            _resident((1, d)),
            _resident((d, 3 * d)),
            _resident((1, 3 * d)),
            _resident((d, d)),
            _resident((1, d)),
        ],
        out_specs=pl.BlockSpec((tm, d), row),
        out_shape=jax.ShapeDtypeStruct((t, d), F32),
        compiler_params=_params(("arbitrary",)),
        name="merge_out",
    )(x2, yc, ya, qkm, mkv, mkv, g1, w_gate, b_gate, w_out, g_post)


FFN_FC = 2816


def _ffn_kernel(x_ref, g2_ref, wu_ref, cw_ref, cb_ref, wd_ref, gp_ref, o_ref,
                ext_ref, tail_ref, *, tm, tiles_per_seq):
    first = pl.program_id(0) % tiles_per_seq == 0
    halo = SUBLANES
    f = wd_ref.shape[0]
    fc = FFN_FC
    x = x_ref[...]
    h = _rms(x, g2_ref[...]).astype(BF16)

    @pl.when(pl.program_id(0) == 0)
    def _():
        tail_ref[...] = jnp.zeros(tail_ref.shape, F32)

    def conv_half(col0, slab0):
        up = _dot(h, wu_ref[:, col0:col0 + fc])
        outs = []
        for s in range(fc // LANES):
            piece = up[:, s * LANES:(s + 1) * LANES]
            cs = slice(col0 + s * LANES, col0 + (s + 1) * LANES)
            slab = slab0 + s
            ext_ref[slab, 0:halo, :] = jnp.where(first, 0.0, tail_ref[slab])
            ext_ref[slab, halo:halo + tm, :] = piece
            tail_ref[slab] = piece[tm - halo:tm, :]
            u = cb_ref[:, cs] + cw_ref[2:3, cs] * piece
            u = u + cw_ref[1:2, cs] * ext_ref[slab, halo - 1:halo - 1 + tm, :]
            u = u + cw_ref[0:1, cs] * ext_ref[slab, halo - 2:halo - 2 + tm, :]
            outs.append(u)
        return jnp.concatenate(outs, axis=1)

    acc = None
    nslab = fc // LANES
    for j in range(f // fc):
        ug = conv_half(j * fc, 2 * j * nslab)
        uv = conv_half(f + j * fc, (2 * j + 1) * nslab)
        act = (ug * jax.nn.sigmoid(ug)) * uv
        part = _dot(act.astype(BF16), wd_ref[j * fc:(j + 1) * fc, :])
        acc = part if acc is None else acc + part
    o_ref[...] = x + _rms(acc, gp_ref[...])


def _ffn(x1, g2, w_up, cw, cb, w_down, g_post, seq, tm):
    t, d = x1.shape
    f = w_down.shape[0]
    fc = FFN_FC
    kern = functools.partial(_ffn_kernel, tm=tm, tiles_per_seq=seq // tm)
    return pl.pallas_call(
        kern,
        grid=(t // tm,),
        in_specs=[
            pl.BlockSpec((tm, d), lambda i: (i, 0)),
            _resident((1, d)),
            _resident(w_up.shape),
            _resident(cw.shape),
            _resident(cb.shape),
            _resident(w_down.shape),
            _resident((1, d)),
        ],
        out_specs=pl.BlockSpec((tm, d), lambda i: (i, 0)),
        out_shape=jax.ShapeDtypeStruct((t, d), F32),
        scratch_shapes=[
            pltpu.VMEM((2 * f // LANES, tm + SUBLANES, LANES), F32),
            pltpu.VMEM((2 * f // LANES, SUBLANES, LANES), F32),
        ],
        compiler_params=_params(("arbitrary",)),
        name="ffn",
    )(x1, g2, w_up, cw, cb, w_down, g_post)


def _pad_rows(a, rows):
    return jnp.pad(a, ((0, rows - a.shape[0]), (0, 0)))


def _pad_cols(a, cols):
    return jnp.pad(a, ((0, 0), (0, cols - a.shape[1])))


def _layer(x2, mem2, batch, seq, n_mem, p):
    d = D_MODEL
    row = lambda v: v.reshape(1, -1)
    w_in = p["w_in"]
    c2 = 2 * d
    o_q, o_qi = c2, c2 + 3 * d
    o_wi = o_qi + IDX_HEADS * IDX_HEAD_DIM
    o_ki = o_wi + IDX_HEADS
    o_qm = o_ki + IDX_HEAD_DIM

    w_glu = w_in[:, :c2].astype(BF16)
    w_qkm = jnp.concatenate([w_in[:, o_q:o_q + 2 * d], w_in[:, o_qm:]], axis=1).astype(BF16)
    w_vt = w_in[:, o_q + 2 * d:o_qi].T.astype(BF16)
    w_idx = jnp.concatenate(
        [w_in[:, o_qi:o_wi],
         _pad_cols(jnp.concatenate([w_in[:, o_ki:o_qm], w_in[:, o_wi:o_ki]], axis=1), LANES)], axis=1)
    w_idx_hi = w_idx.astype(BF16)
    w_idx_lo = (w_idx - w_idx_hi.astype(F32)).astype(BF16)

    g1 = row(p["norm1_pre_g"])
    yg, qkm, vt, qcat, kcat, wi = _proj(x2, g1, w_glu, w_qkm, w_vt, w_idx_hi, w_idx_lo)

    dw_w = _pad_rows(p["conv_dw_w"], 32)
    y_conv = _conv_branch(yg, dw_w, row(p["conv_dw_b"]), row(p["conv_ln_g"]),
                          row(p["conv_ln_b"]), p["conv_pw2"].astype(BF16), seq, ts=512)

    tk = DSA_TK
    tri = (lax.broadcasted_iota(jnp.int32, (tk, tk), 1)
           < lax.broadcasted_iota(jnp.int32, (tk, tk), 0)).astype(BF16)
    wit = wi[:, :IDX_HEADS].T
    y_att = _dsa(qkm, vt, qcat, kcat, wit, tri, batch, seq)

    mkv = _proj_bf16(mem2, row(p["mem_norm_g"]), p["w_mem_kv"].astype(BF16), tm=n_mem)

    x1 = _merge(x2, y_conv, y_att, qkm, mkv, g1, p["w_gate"].astype(BF16), row(p["b_gate"]),
                p["w_out"].astype(BF16), row(p["norm1_post_g"]), seq, n_mem, tm=256)

    x2o = _ffn(x1, row(p["norm2_pre_g"]), p["w_up"].astype(BF16),
               _pad_rows(p["ffn_dw_w"], SUBLANES), row(p["ffn_dw_b"]),
               p["w_down"].astype(BF16), row(p["norm2_post_g"]), seq, tm=512)
    return x2o


def kernel(x, mem, norm1_pre_g, w_in, conv_dw_w, conv_dw_b, conv_ln_g, conv_ln_b, conv_pw2,
           mem_norm_g, w_mem_kv, w_gate, b_gate, w_out, norm1_post_g, norm2_pre_g, w_up,
           ffn_dw_w, ffn_dw_b, w_down, norm2_post_g):
    batch, seq, d = x.shape
    n_mem = mem.shape[1]
    assert d == D_MODEL and seq % DSA_TK == 0 and n_mem % SUBLANES == 0
    names = ("norm1_pre_g", "w_in", "conv_dw_w", "conv_dw_b", "conv_ln_g", "conv_ln_b",
             "conv_pw2", "mem_norm_g", "w_mem_kv", "w_gate", "b_gate", "w_out", "norm1_post_g",
             "norm2_pre_g", "w_up", "ffn_dw_w", "ffn_dw_b", "w_down", "norm2_post_g")
    vals = (norm1_pre_g, w_in, conv_dw_w, conv_dw_b, conv_ln_g, conv_ln_b, conv_pw2,
            mem_norm_g, w_mem_kv, w_gate, b_gate, w_out, norm1_post_g, norm2_pre_g, w_up,
            ffn_dw_w, ffn_dw_b, w_down, norm2_post_g)
    x2 = x.reshape(batch * seq, d)
    mem2 = mem.reshape(batch * n_mem, d)
    for l in range(norm1_pre_g.shape[0]):
        x2 = _layer(x2, mem2, batch, seq, n_mem, {n: v[l] for n, v in zip(names, vals)})
    return x2.reshape(batch, seq, d)
```

```python
import functools

import jax
import jax.numpy as jnp
from jax import lax
from jax.experimental import pallas as pl
from jax.experimental.pallas import tpu as pltpu

EPS = 1e-6
D_MODEL = 1024
CONV_K = 31
ATT_HEADS = 8
ATT_HEAD_DIM = 128
IDX_HEADS = 8
IDX_HEAD_DIM = 64
TOPK_MAX = 256
MEM_HEADS = 4
MEM_HEAD_DIM = 256

LANES = 128
SUBLANES = 8
VMEM_LIMIT_BYTES = 56 * 1024 * 1024

F32 = jnp.float32
BF16 = jnp.bfloat16
NEG_INF = float("-inf")
POS_INF = float("inf")


def _params(semantics):
    return pltpu.CompilerParams(dimension_semantics=semantics,
                                vmem_limit_bytes=VMEM_LIMIT_BYTES)


def _resident(shape):
    return pl.BlockSpec(shape, lambda *_: (0,) * len(shape), pipeline_mode=pl.Buffered(1))


def _dot(a, b):
    return jnp.dot(a, b, preferred_element_type=F32)


def _dot_nt(a, b):
    return lax.dot_general(a, b, (((1,), (1,)), ((), ())), preferred_element_type=F32)


def _rms(x, g):
    return x * lax.rsqrt(jnp.mean(x * x, axis=-1, keepdims=True) + EPS) * g


def _split_bf16(v):
    hi = v.astype(BF16).astype(F32)
    lo = (v - hi).astype(BF16).astype(F32)
    return hi, lo


PROJ_TN = 512
Q_PRESCALE = ATT_HEAD_DIM ** -0.5 * 1.4426950408889634
IDX_CAT = 2 * LANES


def _proj_kernel(x_ref, g_ref, wglu_ref, wqkm_ref, wvt_ref, whi_ref, wlo_ref,
                 yg_ref, qkm_ref, vt_ref, qcat_ref, kcat_ref, wi_ref):
    h = _rms(x_ref[...], g_ref[...])
    h_hi = h.astype(BF16)
    h_lo = (h - h_hi.astype(F32)).astype(BF16)

    c = yg_ref.shape[1]
    for j in range(c // PROJ_TN):
        a = _dot(h_hi, wglu_ref[:, j * PROJ_TN:(j + 1) * PROJ_TN])
        gate = _dot(h_hi, wglu_ref[:, c + j * PROJ_TN:c + (j + 1) * PROJ_TN])
        yg_ref[:, j * PROJ_TN:(j + 1) * PROJ_TN] = a * jax.nn.sigmoid(gate)

    for j in range(qkm_ref.shape[1] // PROJ_TN):
        cs = slice(j * PROJ_TN, (j + 1) * PROJ_TN)
        r = _dot(h_hi, wqkm_ref[:, cs])
        if (j + 1) * PROJ_TN <= ATT_HEADS * ATT_HEAD_DIM:
            r = r * Q_PRESCALE
        qkm_ref[:, cs] = r.astype(BF16)

    for j in range(vt_ref.shape[1] // PROJ_TN):
        cs = slice(j * PROJ_TN, (j + 1) * PROJ_TN)
        vt_ref[0, cs, :] = _dot_nt(wvt_ref[cs, :], h_hi).astype(BF16)

    whi = whi_ref[...]
    out = _dot(h_hi, whi) + _dot(h_lo, whi) + _dot(h_hi, wlo_ref[...])
    low_half = lax.broadcasted_iota(jnp.int32, (1, LANES), 1) < IDX_HEAD_DIM
    npair = IDX_HEADS // 2
    for p in range(npair):
        hi, lo = _split_bf16(out[:, p * LANES:(p + 1) * LANES])
        hi_sw = pltpu.roll(hi, IDX_HEAD_DIM, 1)
        lo_sw = pltpu.roll(lo, IDX_HEAD_DIM, 1)
        base = 2 * p * IDX_CAT
        qcat_ref[:, base:base + LANES] = jnp.where(low_half, hi, lo_sw).astype(BF16)
        qcat_ref[:, base + LANES:base + 2 * LANES] = jnp.where(low_half, hi, 0.0).astype(BF16)
        qcat_ref[:, base + 2 * LANES:base + 3 * LANES] = jnp.where(low_half, hi_sw, lo).astype(BF16)
        qcat_ref[:, base + 3 * LANES:base + 4 * LANES] = jnp.where(low_half, hi_sw, 0.0).astype(BF16)
    kw = out[:, npair * LANES:(npair + 1) * LANES]
    k_hi, k_lo = _split_bf16(kw)
    kcat_ref[:, 0:LANES] = jnp.where(low_half, k_hi, pltpu.roll(k_hi, IDX_HEAD_DIM, 1)).astype(BF16)
    kcat_ref[:, LANES:2 * LANES] = jnp.where(low_half, k_lo, 0.0).astype(BF16)
    wi_ref[...] = pltpu.roll(kw, IDX_HEAD_DIM, 1)


def _proj(x2, g, w_glu, w_qkm, w_vt, w_hi, w_lo):
    tm = DSA_TK
    t, d = x2.shape
    c = w_glu.shape[1] // 2
    nq = w_qkm.shape[1]
    nv = w_vt.shape[0]
    row = lambda i: (i, 0)
    return pl.pallas_call(
        _proj_kernel,
        grid=(t // tm,),
        in_specs=[
            pl.BlockSpec((tm, d), row),
            _resident((1, d)),
            _resident(w_glu.shape),
            _resident(w_qkm.shape),
            _resident(w_vt.shape),
            _resident(w_hi.shape),
            _resident(w_lo.shape),
        ],
        out_specs=[
            pl.BlockSpec((tm, c), row),
            pl.BlockSpec((tm, nq), row),
            pl.BlockSpec((1, nv, tm), lambda i: (i, 0, 0)),
            pl.BlockSpec((tm, IDX_HEADS * IDX_CAT), row),
            pl.BlockSpec((tm, IDX_CAT), row),
            pl.BlockSpec((tm, LANES), row),
        ],
        out_shape=[
            jax.ShapeDtypeStruct((t, c), F32),
            jax.ShapeDtypeStruct((t, nq), BF16),
            jax.ShapeDtypeStruct((t // tm, nv, tm), BF16),
            jax.ShapeDtypeStruct((t, IDX_HEADS * IDX_CAT), BF16),
            jax.ShapeDtypeStruct((t, IDX_CAT), BF16),
            jax.ShapeDtypeStruct((t, LANES), F32),
        ],
        compiler_params=_params(("arbitrary",)),
        name="proj",
    )(x2, g, w_glu, w_qkm, w_vt, w_hi, w_lo)


def _proj_bf16_kernel(x_ref, g_ref, w_ref, o_ref):
    h = _rms(x_ref[...], g_ref[...]).astype(BF16)
    o_ref[...] = _dot(h, w_ref[...]).astype(BF16)


def _proj_bf16(x2, g, w, tm):
    t, d = x2.shape
    n = w.shape[1]
    return pl.pallas_call(
        _proj_bf16_kernel,
        grid=(t // tm,),
        in_specs=[
            pl.BlockSpec((tm, d), lambda i: (i, 0)),
            _resident((1, d)),
            _resident(w.shape),
        ],
        out_specs=pl.BlockSpec((tm, n), lambda i: (i, 0)),
        out_shape=jax.ShapeDtypeStruct((t, n), BF16),
        compiler_params=_params(("arbitrary",)),
        name="proj_mem",
    )(x2, g, w)


CONV_HALO = 32
CONV_ROWS = 64


def _conv_module(y_ref, dww_ref, dwb_ref, lng_ref, lnb_ref, pw2_ref, ext_ref, cv_ref, ts, first):
    c = y_ref.shape[1]

    @pl.when(first)
    def _():
        ext_ref[:, 0:CONV_HALO, :] = jnp.zeros((c // LANES, CONV_HALO, LANES), F32)

    @pl.when(jnp.logical_not(first))
    def _():
        ext_ref[:, 0:CONV_HALO, :] = ext_ref[:, ts:ts + CONV_HALO, :]

    off0 = CONV_HALO - (CONV_K - 1)
    for cg in range(c // LANES):
        cs = slice(cg * LANES, (cg + 1) * LANES)
        ext_ref[cg, CONV_HALO:CONV_HALO + ts, :] = y_ref[:, cs]
        for rb in range(ts // CONV_ROWS):
            r0 = rb * CONV_ROWS
            acc = jnp.broadcast_to(dwb_ref[:, cs], (CONV_ROWS, LANES))
            for j in range(CONV_K):
                a0 = r0 + off0 + j
                acc = acc + dww_ref[j:j + 1, cs] * ext_ref[cg, a0:a0 + CONV_ROWS, :]
            cv_ref[r0:r0 + CONV_ROWS, cs] = acc

    y = cv_ref[...]
    mu = jnp.mean(y, axis=-1, keepdims=True)
    yc = y - mu
    var = jnp.mean(yc * yc, axis=-1, keepdims=True)
    z = yc * lax.rsqrt(var + EPS) * lng_ref[...] + lnb_ref[...]
    z = z * jax.nn.sigmoid(z)
    return _dot(z.astype(BF16), pw2_ref[...])


DSA_TQ = 256
DSA_TK = 512
DSA_NACC = 4
ATT_NACC = 2
DSA_HG = 2
BISECT_LINEAR = 24
BISECT_UNROLL = 2
BISECT_MAX = 40


def _ordered_key(f):
    b = lax.bitcast_convert_type(f, jnp.int32)
    return b ^ ((b >> 31) & jnp.int32(0x7FFFFFFF))


def _from_ordered_key(k):
    return lax.bitcast_convert_type(k ^ ((k >> 31) & jnp.int32(0x7FFFFFFF)), F32)


def _dsa_kernel(q_ref, k_ref, vt_ref, qcat_ref, kcat_ref, wit_ref, tri_ref, o_ref,
                sc_ref, lg0_ref, lg1_ref, acc_ref, *, topk):
    lg_refs = (lg0_ref, lg1_ref)
    tq, tk = DSA_TQ, DSA_TK
    qt = pl.program_id(1)
    nck = ((qt + 1) * tq + tk - 1) // tk
    idx_scale = (IDX_HEADS ** -0.5) * (IDX_HEAD_DIM ** -0.5)
    kf = float(topk)

    q_pos = qt * tq + lax.broadcasted_iota(jnp.int32, (1, tq), 1)
    key_off = lax.broadcasted_iota(jnp.int32, (tk, 1), 0)

    def chunk_loop(body, init):
        def quad(j, carry):
            for u in range(4):
                carry = body(4 * j + u, carry)
            return carry
        carry = lax.fori_loop(0, lax.shift_right_logical(nck, 2), quad, init)
        c2 = nck & ~3
        carry = lax.cond((nck & 2) == 2, lambda cr: body(c2 + 1, body(c2, cr)), lambda cr: cr, carry)
        return lax.cond((nck & 1) == 1, lambda cr: body(nck - 1, cr), lambda cr: cr, carry)

    def score_chunk(c, carry):
        mn, mx = carry
        r0 = pl.multiple_of(c * tk, tk)
        kc = kcat_ref[pl.ds(r0, tk), :]
        acc = jnp.zeros((tk, tq), F32)
        for h in range(IDX_HEADS):
            d = _dot_nt(kc, qcat_ref[:, h * IDX_CAT:(h + 1) * IDX_CAT])
            acc = acc + jnp.maximum(d, 0.0) * wit_ref[h:h + 1, :]
        s = acc * idx_scale
        causal = (c * tk + key_off) <= q_pos
        lo_s = jnp.where(causal, s, POS_INF)
        hi_s = jnp.where(causal, s, NEG_INF)
        sc_ref[c] = hi_s
        mn = jnp.minimum(mn, jnp.min(lo_s, axis=0, keepdims=True))
        mx = jnp.maximum(mx, jnp.max(hi_s, axis=0, keepdims=True))
        return mn, mx

    row_min, row_max = chunk_loop(
        score_chunk, (jnp.full((1, tq), POS_INF, F32), jnp.full((1, tq), NEG_INF, F32)))

    def count_rows(cand, strict):
        cb = jnp.broadcast_to(cand, (SUBLANES, tq))

        def body(c, parts):
            parts = list(parts)
            for i in range(tk // SUBLANES):
                s = sc_ref[c, i * SUBLANES:(i + 1) * SUBLANES, :]
                hit = (s > cb) if strict else (s >= cb)
                a = i % DSA_NACC
                parts[a] = jnp.where(hit, parts[a] + 1.0, parts[a])
            return tuple(parts)

        parts = lax.fori_loop(0, nck, body,
                              tuple(jnp.zeros((SUBLANES, tq), F32) for _ in range(DSA_NACC)))
        return jnp.sum(functools.reduce(jnp.add, parts), axis=0, keepdims=True)

    n_valid = (q_pos + 1).astype(F32)
    few = n_valid <= kf
    zero = jnp.zeros((1, tq), F32)
    cnt_ge0 = count_rows(zero, strict=False)
    cnt_gt0 = count_rows(zero, strict=True)
    pos = cnt_ge0 >= kf
    lo0 = jnp.where(pos, 0.0, row_min)
    cnt_lo0 = jnp.where(pos, cnt_ge0, n_valid)
    hi0 = jnp.where(pos, _from_ordered_key(_ordered_key(row_max) + 1), 0.0)
    cnt_hi0 = jnp.where(pos, 0.0, cnt_ge0)

    def narrow(mid, cnt, lo, hi, cnt_lo, cnt_hi, done):
        ge = (cnt >= kf) & (done < 0.5)
        lt = (cnt < kf) & (done < 0.5)
        return (jnp.where(ge, mid, lo), jnp.where(lt, mid, hi),
                jnp.where(ge, cnt, cnt_lo), jnp.where(lt, cnt, cnt_hi))

    def any_active(done):
        return jnp.max(1.0 - done).astype(jnp.int32)

    zero_thr = (cnt_gt0 < kf) & pos
    settled = jnp.where(few | zero_thr, 1.0, 0.0)

    def pinned(cnt_lo, cnt_hi):
        return (cnt_lo == kf) | (cnt_hi == kf - 1.0)

    def lin_done(lo, hi, mid, cnt_lo, cnt_hi, done):
        return jnp.where(pinned(cnt_lo, cnt_hi) | (mid == lo) | (mid == hi), 1.0, done)

    def lin_trip(state):
        it, _, lo, hi, mid, cnt_lo, cnt_hi, done = state
        for _ in range(BISECT_UNROLL):
            cnt = count_rows(mid, strict=False)
            lo, hi, cnt_lo, cnt_hi = narrow(mid, cnt, lo, hi, cnt_lo, cnt_hi, done)
            mid = 0.5 * lo + 0.5 * hi
            done = lin_done(lo, hi, mid, cnt_lo, cnt_hi, done)
        return it + BISECT_UNROLL, any_active(done), lo, hi, mid, cnt_lo, cnt_hi, done

    mid0 = 0.5 * lo0 + 0.5 * hi0
    done0 = lin_done(lo0, hi0, mid0, cnt_lo0, cnt_hi0, settled)
    _, _, lo1, hi1, _, cnt_lo1, cnt_hi1, _ = lax.while_loop(
        lambda st: jnp.logical_and(st[1] > 0, st[0] < BISECT_LINEAR), lin_trip,
        (jnp.int32(0), any_active(done0), lo0, hi0, mid0, cnt_lo0, cnt_hi0, done0))

    def lat_done(lo, hi, cnt_lo, cnt_hi, done):
        adjacent = _ordered_key(hi) <= _ordered_key(lo) + 1
        return jnp.where(pinned(cnt_lo, cnt_hi) | adjacent, 1.0, done)

    def lat_trip(state):
        it, _, lo, hi, cnt_lo, cnt_hi, done = state
        klo, khi = _ordered_key(lo), _ordered_key(hi)
        mid = _from_ordered_key((klo >> 1) + (khi >> 1) + (klo & khi & 1))
        cnt = count_rows(mid, strict=False)
        lo, hi, cnt_lo, cnt_hi = narrow(mid, cnt, lo, hi, cnt_lo, cnt_hi, done)
        done = lat_done(lo, hi, cnt_lo, cnt_hi, done)
        return it + 1, any_active(done), lo, hi, cnt_lo, cnt_hi, done

    done1 = lat_done(lo1, hi1, cnt_lo1, cnt_hi1, settled)
    _, _, lo2, hi2, cnt_lo2, cnt_hi2, _ = lax.while_loop(
        lambda st: jnp.logical_and(st[1] > 0, st[0] < BISECT_MAX), lat_trip,
        (jnp.int32(0), any_active(done1), lo1, hi1, cnt_lo1, cnt_hi1, done1))

    def max_below(bound):
        bb = jnp.broadcast_to(bound, (SUBLANES, tq))

        def body(c, parts):
            parts = list(parts)
            for i in range(tk // SUBLANES):
                s = sc_ref[c, i * SUBLANES:(i + 1) * SUBLANES, :]
                a = i % DSA_NACC
                parts[a] = jnp.maximum(parts[a], jnp.where(s < bb, s, NEG_INF))
            return tuple(parts)

        parts = lax.fori_loop(0, nck, body,
                              tuple(jnp.full((SUBLANES, tq), NEG_INF, F32) for _ in range(DSA_NACC)))
        return jnp.max(functools.reduce(jnp.maximum, parts), axis=0, keepdims=True)

    top_below = max_below(hi2)
    by_lo = cnt_lo2 == kf
    by_hi = jnp.logical_not(by_lo) & (cnt_hi2 == kf - 1.0)
    thr = jnp.where(by_hi, top_below, lo2)
    need = jnp.where(by_lo, kf, kf - cnt_hi2)
    thr = jnp.where(few, NEG_INF, jnp.where(zero_thr, 0.0, thr))
    need = jnp.where(few, 0.0, jnp.where(zero_thr, kf - cnt_gt0, need))

    def bias_chunk(c, seen):
        s = sc_ref[c]
        eq = s == thr
        eqf = jnp.where(eq, 1.0, 0.0)
        before = _dot(tri_ref[...], eqf.astype(BF16)) + seen
        keep = (s > thr) | (eq & (before < need))
        sc_ref[c] = jnp.where(keep, 0.0, NEG_INF)
        return seen + jnp.sum(eqf, axis=0, keepdims=True)

    chunk_loop(bias_chunk, jnp.zeros((1, tq), F32))

    def fold_rows(x, op, parts):
        parts = list(parts)
        for i in range(tk // SUBLANES):
            a = i % ATT_NACC
            parts[a] = op(parts[a], x[i * SUBLANES:(i + 1) * SUBLANES, :])
        return tuple(parts)

    def init_parts(value):
        return tuple(tuple(jnp.full((SUBLANES, tq), value, F32) for _ in range(ATT_NACC))
                     for _ in range(DSA_HG))

    def head_cols(grp, i):
        h = grp * DSA_HG + i
        return slice(h * ATT_HEAD_DIM, (h + 1) * ATT_HEAD_DIM)

    def make_body(grp_l, grp_p, m_rows):
        def body(c, carry):
            mparts, lparts = carry
            r0 = pl.multiple_of(c * tk, tk)
            if grp_l is not None:
                new_m = []
                for i in range(DSA_HG):
                    hs = head_cols(grp_l, i)
                    l = _dot_nt(k_ref[pl.ds(r0, tk), hs], q_ref[:, hs]) + sc_ref[c]
                    lg_refs[grp_l % 2][i, c] = l
                    new_m.append(fold_rows(l, jnp.maximum, mparts[i]))
                mparts = tuple(new_m)
            if grp_p is not None:
                new_l = []
                for i in range(DSA_HG):
                    hs = head_cols(grp_p, i)
                    p = jnp.exp2(lg_refs[grp_p % 2][i, c] - m_rows[i])
                    new_l.append(fold_rows(p, jnp.add, lparts[i]))
                    acc_ref[i] = acc_ref[i] + _dot(vt_ref[c, hs, :], p.astype(BF16))
                lparts = tuple(new_l)
            return mparts, lparts
        return body

    ngrp = ATT_HEADS // DSA_HG
    m_rows = None
    for phase in range(ngrp + 1):
        grp_l = phase if phase < ngrp else None
        grp_p = phase - 1 if phase > 0 else None
        if grp_p is not None:
            acc_ref[...] = jnp.zeros(acc_ref.shape, F32)
        mparts, lparts = chunk_loop(make_body(grp_l, grp_p, m_rows),
                                    (init_parts(NEG_INF), init_parts(0.0)))
        if grp_p is not None:
            for i in range(DSA_HG):
                lsum = jnp.sum(functools.reduce(jnp.add, lparts[i]), axis=0, keepdims=True)
                o_ref[:, head_cols(grp_p, i)] = (acc_ref[i] / lsum).T
        if grp_l is not None:
            m_rows = [jnp.max(functools.reduce(jnp.maximum, mp), axis=0, keepdims=True)
                      for mp in mparts]


def _dsa(qkm, vt, qcat, kcat, wit, tri, batch, seq):
    tq, tk = DSA_TQ, DSA_TK
    d = D_MODEL
    nq = seq // tq
    nc = seq // tk
    topk = min(TOPK_MAX, seq // 4)
    kern = functools.partial(_dsa_kernel, topk=topk)
    qrow = lambda b, i: (b * nq + i, 0)
    return pl.pallas_call(
        kern,
        grid=(batch, nq),
        in_specs=[
            pl.BlockSpec((tq, d), qrow),
            pl.BlockSpec((seq, d), lambda b, i: (b, 1), pipeline_mode=pl.Buffered(1)),
            pl.BlockSpec((nc, d, tk), lambda b, i: (b, 0, 0), pipeline_mode=pl.Buffered(1)),
            pl.BlockSpec((tq, qcat.shape[1]), qrow),
            pl.BlockSpec((seq, kcat.shape[1]), lambda b, i: (b, 0)),
            pl.BlockSpec((IDX_HEADS, tq), lambda b, i: (0, b * nq + i)),
            _resident((tk, tk)),
        ],
        out_specs=pl.BlockSpec((tq, d), qrow),
        out_shape=jax.ShapeDtypeStruct((batch * seq, d), F32),
        scratch_shapes=[
            pltpu.VMEM((nc, tk, tq), F32),
            pltpu.VMEM((DSA_HG, nc, tk, tq), F32),
            pltpu.VMEM((DSA_HG, nc, tk, tq), F32),
            pltpu.VMEM((DSA_HG, ATT_HEAD_DIM, tq), F32),
        ],
        compiler_params=_params(("arbitrary", "arbitrary")),
        name="dsa",
    )(qkm, qkm, vt, qcat, kcat, wit, tri)


def _mem_attention(qm_ref, mk_ref, mv_ref):
    scale = MEM_HEAD_DIM ** -0.5
    outs = []
    for h in range(MEM_HEADS):
        hs = slice(h * MEM_HEAD_DIM, (h + 1) * MEM_HEAD_DIM)
        l = _dot_nt(qm_ref[:, hs], mk_ref[:, hs]) * scale
        p = jnp.exp(l - jnp.max(l, axis=1, keepdims=True))
        acc = _dot(p.astype(BF16), mv_ref[:, hs])
        outs.append(acc / jnp.sum(p, axis=1, keepdims=True))
    return jnp.concatenate(outs, axis=1)


def _merge_kernel(x_ref, yg_ref, ya_ref, qm_ref, mk_ref, mv_ref, g1_ref, wg_ref, bg_ref,
                  wo_ref, gp_ref, dww_ref, dwb_ref, lng_ref, lnb_ref, pw2_ref, o_ref,
                  ext_ref, cv_ref, *, tm, tiles_per_seq):
    d = x_ref.shape[1]
    x = x_ref[...]
    h = _rms(x, g1_ref[...]).astype(BF16)
    first = pl.program_id(0) % tiles_per_seq == 0
    y_conv = _conv_module(yg_ref, dww_ref, dwb_ref, lng_ref, lnb_ref, pw2_ref, ext_ref, cv_ref,
                          tm, first)
    branches = (y_conv, ya_ref[...], _mem_attention(qm_ref, mk_ref, mv_ref))
    merged = jnp.zeros(x.shape, F32)
    for br, y in enumerate(branches):
        cs = slice(br * d, (br + 1) * d)
        gate = jax.nn.sigmoid(_dot(h, wg_ref[:, cs]) + bg_ref[:, cs])
        merged = merged + gate * y
    out = _dot(merged.astype(BF16), wo_ref[...])
    o_ref[...] = x + _rms(out, gp_ref[...])


def _merge(x2, yg, ya, qkm, mkv, g1, w_gate, b_gate, w_out, g_post, dw_w, dw_b, ln_g, ln_b, pw2,
           seq, n_mem, tm):
    t, d = x2.shape
    row = lambda i: (i, 0)
    tiles_per_seq = seq // tm
    kern = functools.partial(_merge_kernel, tm=tm, tiles_per_seq=tiles_per_seq)
    return pl.pallas_call(
        kern,
        grid=(t // tm,),
        in_specs=[
            pl.BlockSpec((tm, d), row), pl.BlockSpec((tm, d), row),
            pl.BlockSpec((tm, d), row),
            pl.BlockSpec((tm, d), lambda i: (i, 2)),
            pl.BlockSpec((n_mem, d), lambda i: (i // tiles_per_seq, 0)),
            pl.BlockSpec((n_mem, d), lambda i: (i // tiles_per_seq, 1)),
            _resident((1, d)),
            _resident((d, 3 * d)),
            _resident((1, 3 * d)),
            _resident((d, d)),
            _resident((1, d)),
            _resident(dw_w.shape),
            _resident((1, d)),
            _resident((1, d)),
            _resident((1, d)),
            _resident((d, d)),
        ],
        out_specs=pl.BlockSpec((tm, d), row),
        out_shape=jax.ShapeDtypeStruct((t, d), F32),
        scratch_shapes=[pltpu.VMEM((d // LANES, tm + CONV_HALO, LANES), F32),
                        pltpu.VMEM((tm, d), F32)],
        compiler_params=_params(("arbitrary",)),
        name="merge_out",
    )(x2, yg, ya, qkm, mkv, mkv, g1, w_gate, b_gate, w_out, g_post, dw_w, dw_b, ln_g, ln_b, pw2)


FFN_FC = 2816


def _ffn_kernel(x_ref, g2_ref, wu_ref, cw_ref, cb_ref, wd_ref, gp_ref, o_ref,
                ext_ref, tail_ref, *, tm, tiles_per_seq):
    first = pl.program_id(0) % tiles_per_seq == 0
    halo = SUBLANES
    f = wd_ref.shape[0]
    fc = FFN_FC
    x = x_ref[...]
    h = _rms(x, g2_ref[...]).astype(BF16)

    @pl.when(pl.program_id(0) == 0)
    def _():
        tail_ref[...] = jnp.zeros(tail_ref.shape, F32)

    def conv_half(col0, slab0):
        up = _dot(h, wu_ref[:, col0:col0 + fc])
        outs = []
        for s in range(fc // LANES):
            piece = up[:, s * LANES:(s + 1) * LANES]
            cs = slice(col0 + s * LANES, col0 + (s + 1) * LANES)
            slab = slab0 + s
            ext_ref[slab, 0:halo, :] = jnp.where(first, 0.0, tail_ref[slab])
            ext_ref[slab, halo:halo + tm, :] = piece
            tail_ref[slab] = piece[tm - halo:tm, :]
            u = cb_ref[:, cs] + cw_ref[2:3, cs] * piece
            u = u + cw_ref[1:2, cs] * ext_ref[slab, halo - 1:halo - 1 + tm, :]
            u = u + cw_ref[0:1, cs] * ext_ref[slab, halo - 2:halo - 2 + tm, :]
            outs.append(u)
        return jnp.concatenate(outs, axis=1)

    acc = None
    nslab = fc // LANES
    for j in range(f // fc):
        ug = conv_half(j * fc, 2 * j * nslab)
        uv = conv_half(f + j * fc, (2 * j + 1) * nslab)
        act = (ug * jax.nn.sigmoid(ug)) * uv
        part = _dot(act.astype(BF16), wd_ref[j * fc:(j + 1) * fc, :])
        acc = part if acc is None else acc + part
    o_ref[...] = x + _rms(acc, gp_ref[...])


def _ffn(x1, g2, w_up, cw, cb, w_down, g_post, seq, tm):
    t, d = x1.shape
    f = w_down.shape[0]
    fc = FFN_FC
    kern = functools.partial(_ffn_kernel, tm=tm, tiles_per_seq=seq // tm)
    return pl.pallas_call(
        kern,
        grid=(t // tm,),
        in_specs=[
            pl.BlockSpec((tm, d), lambda i: (i, 0)),
            _resident((1, d)),
            _resident(w_up.shape),
            _resident(cw.shape),
            _resident(cb.shape),
            _resident(w_down.shape),
            _resident((1, d)),
        ],
        out_specs=pl.BlockSpec((tm, d), lambda i: (i, 0)),
        out_shape=jax.ShapeDtypeStruct((t, d), F32),
        scratch_shapes=[
            pltpu.VMEM((2 * f // LANES, tm + SUBLANES, LANES), F32),
            pltpu.VMEM((2 * f // LANES, SUBLANES, LANES), F32),
        ],
        compiler_params=_params(("arbitrary",)),
        name="ffn",
    )(x1, g2, w_up, cw, cb, w_down, g_post)


def _pad_rows(a, rows):
    return jnp.pad(a, ((0, rows - a.shape[0]), (0, 0)))


def _pad_cols(a, cols):
    return jnp.pad(a, ((0, 0), (0, cols - a.shape[1])))


def _layer(x2, mem2, batch, seq, n_mem, p):
    d = D_MODEL
    row = lambda v: v.reshape(1, -1)
    w_in = p["w_in"]
    c2 = 2 * d
    o_q, o_qi = c2, c2 + 3 * d
    o_wi = o_qi + IDX_HEADS * IDX_HEAD_DIM
    o_ki = o_wi + IDX_HEADS
    o_qm = o_ki + IDX_HEAD_DIM

    w_glu = w_in[:, :c2].astype(BF16)
    w_qkm = jnp.concatenate([w_in[:, o_q:o_q + 2 * d], w_in[:, o_qm:]], axis=1).astype(BF16)
    w_vt = w_in[:, o_q + 2 * d:o_qi].T.astype(BF16)
    w_idx = jnp.concatenate(
        [w_in[:, o_qi:o_wi],
         _pad_cols(jnp.concatenate([w_in[:, o_ki:o_qm], w_in[:, o_wi:o_ki]], axis=1), LANES)], axis=1)
    w_idx_hi = w_idx.astype(BF16)
    w_idx_lo = (w_idx - w_idx_hi.astype(F32)).astype(BF16)

    g1 = row(p["norm1_pre_g"])
    yg, qkm, vt, qcat, kcat, wi = _proj(x2, g1, w_glu, w_qkm, w_vt, w_idx_hi, w_idx_lo)

    tk = DSA_TK
    tri = (lax.broadcasted_iota(jnp.int32, (tk, tk), 1)
           < lax.broadcasted_iota(jnp.int32, (tk, tk), 0)).astype(BF16)
    wit = wi[:, :IDX_HEADS].T
    y_att = _dsa(qkm, vt, qcat, kcat, wit, tri, batch, seq)

    mkv = _proj_bf16(mem2, row(p["mem_norm_g"]), p["w_mem_kv"].astype(BF16), tm=n_mem)

    x1 = _merge(x2, yg, y_att, qkm, mkv, g1, p["w_gate"].astype(BF16), row(p["b_gate"]),
                p["w_out"].astype(BF16), row(p["norm1_post_g"]),
                _pad_rows(p["conv_dw_w"], 32), row(p["conv_dw_b"]), row(p["conv_ln_g"]),
                row(p["conv_ln_b"]), p["conv_pw2"].astype(BF16), seq, n_mem, tm=512)

    x2o = _ffn(x1, row(p["norm2_pre_g"]), p["w_up"].astype(BF16),
               _pad_rows(p["ffn_dw_w"], SUBLANES), row(p["ffn_dw_b"]),
               p["w_down"].astype(BF16), row(p["norm2_post_g"]), seq, tm=512)
    return x2o


def kernel(x, mem, norm1_pre_g, w_in, conv_dw_w, conv_dw_b, conv_ln_g, conv_ln_b, conv_pw2,
           mem_norm_g, w_mem_kv, w_gate, b_gate, w_out, norm1_post_g, norm2_pre_g, w_up,
           ffn_dw_w, ffn_dw_b, w_down, norm2_post_g):
    batch, seq, d = x.shape
    n_mem = mem.shape[1]
    assert d == D_MODEL and seq % DSA_TK == 0 and n_mem % SUBLANES == 0
    names = ("norm1_pre_g", "w_in", "conv_dw_w", "conv_dw_b", "conv_ln_g", "conv_ln_b",
             "conv_pw2", "mem_norm_g", "w_mem_kv", "w_gate", "b_gate", "w_out", "norm1_post_g",
             "norm2_pre_g", "w_up", "ffn_dw_w", "ffn_dw_b", "w_down", "norm2_post_g")
    vals = (norm1_pre_g, w_in, conv_dw_w, conv_dw_b, conv_ln_g, conv_ln_b, conv_pw2,
            mem_norm_g, w_mem_kv, w_gate, b_gate, w_out, norm1_post_g, norm2_pre_g, w_up,
            ffn_dw_w, ffn_dw_b, w_down, norm2_post_g)
    x2 = x.reshape(batch * seq, d)
    mem2 = mem.reshape(batch * n_mem, d)
    for l in range(norm1_pre_g.shape[0]):
        x2 = _layer(x2, mem2, batch, seq, n_mem, {n: v[l] for n, v in zip(names, vals)})
    return x2.reshape(batch, seq, d)
```
